```python
import jax, jax.numpy as jnp
from jax import lax
import numpy as np

D_MODEL = 4096
BATCH = 2
SEQ = 8192
DEPTH = 4
DEC_BATCH = 16
DEC_SEQ = 64
PAST_LEN = 1024

CHUNK = 64
Q_BLOCK = 128
MIX = D_MODEL
RW_WIDTH = MIX // 2
RW_HD = 64
RW_H = RW_WIDTH // RW_HD
W_LORA = 96
A_LORA = 96
G_LORA = 256
RW_COLS = 3 * RW_WIDTH + W_LORA + A_LORA + G_LORA
V_HD = 128
MLA_H = (MIX - RW_WIDTH) // V_HD
QK_NOPE = 128
QK_ROPE = 64
QK_DIM = QK_NOPE + QK_ROPE
Q_LORA = 768
KV_LORA = 512
MLA_COLS = Q_LORA + KV_LORA + QK_ROPE
IN_COLS = RW_COLS + MLA_COLS
D_FF = -(-8 * D_MODEL // 768) * 256
ROPE_THETA = 10000.0
NORM_EPS = 1e-6
LNX_EPS = 64e-5
QK_SCALE = QK_DIM ** -0.5

kernel_name = "hymba_rwkv7_mla_streaming_step"

F32 = jnp.float32


def rmsnorm(x, g):
    x32 = x.astype(F32)
    y = x32 * lax.rsqrt(jnp.mean(x32 * x32, axis=-1, keepdims=True) + NORM_EPS)
    return (y * g.astype(F32)).astype(x.dtype)


def rope_cos_sin(positions):
    half = QK_ROPE // 2
    inv_freq = ROPE_THETA ** (-jnp.arange(half, dtype=F32) / half)
    ang = positions.astype(F32)[:, None] * inv_freq[None, :]
    return jnp.cos(ang), jnp.sin(ang)


def apply_rope(x, cos, sin):
    half = QK_ROPE // 2
    shp = (1, cos.shape[0]) + (1,) * (x.ndim - 3) + (half,)
    c, s = cos.reshape(shp), sin.reshape(shp)
    x32 = x.astype(F32)
    x1, x2 = x32[..., :half], x32[..., half:]
    return jnp.concatenate([x1 * c - x2 * s, x2 * c + x1 * s], axis=-1).astype(x.dtype)


def qk_norm(x, g):
    g_full = jnp.concatenate([g[:QK_NOPE], g[QK_NOPE:], g[QK_NOPE:]]).astype(F32)
    x32 = x.astype(F32)
    y = x32 * lax.rsqrt(jnp.mean(x32 * x32, axis=-1, keepdims=True) + NORM_EPS)
    return (y * g_full).astype(x.dtype)


def wkv7_scan(s0, r, decay, k, v, kk, a):
    def step(S, inp):
        r_t, w_t, k_t, v_t, kk_t, a_t = inp
        sa = jnp.einsum('bhvk,bhk->bhv', S, -kk_t)
        S = (S * w_t[:, :, None, :]
             + sa[..., None] * (kk_t * a_t)[:, :, None, :]
             + v_t[..., None] * k_t[:, :, None, :])
        y = jnp.einsum('bhvk,bhk->bhv', S, r_t)
        return S, y
    xs = tuple(t.transpose(1, 0, 2, 3) for t in (r, decay, k, v, kk, a))
    s_fin, ys = lax.scan(step, s0, xs)
    return ys.transpose(1, 0, 2, 3), s_fin


def rwkv7_time_mix(p_rw, shift_prev, wkv_prev, mu, w0, w2, a0, a2, g2, k_k, k_a, r_k, lnx_w, lnx_b):
    B, T, _ = p_rw.shape
    prev = jnp.concatenate([shift_prev[:, None, :], p_rw[:, :-1, :]], axis=1)
    xs = p_rw + (prev - p_rw) * mu
    split_at = (RW_WIDTH, 2 * RW_WIDTH, 3 * RW_WIDTH, 3 * RW_WIDTH + W_LORA, 3 * RW_WIDTH + W_LORA + A_LORA)
    r, k, v, wd, ad, gd = jnp.split(xs, split_at, axis=-1)
    w_log = -jax.nn.softplus(-(w0 + jnp.tanh(wd) @ w2).astype(F32)) - 0.5
    decay = jnp.exp(-jnp.exp(w_log))
    a = jax.nn.sigmoid((a0 + ad @ a2).astype(F32))
    g = jax.nn.sigmoid(gd) @ g2
    heads = lambda t: t.reshape(B, T, RW_H, RW_HD)
    kk = heads((k * k_k).astype(F32))
    kk = kk * lax.rsqrt(jnp.maximum(jnp.sum(kk * kk, axis=-1, keepdims=True), 1e-24))
    k_mod = k.astype(F32) * (1.0 + (a - 1.0) * k_a.astype(F32))
    rh, kh, vh = heads(r.astype(F32)), heads(k_mod), heads(v.astype(F32))
    y, wkv_new = wkv7_scan(wkv_prev.astype(F32), rh, heads(decay), kh, vh, kk, heads(a))
    mean = jnp.mean(y, axis=-1, keepdims=True)
    var = jnp.mean(jnp.square(y - mean), axis=-1, keepdims=True)
    y = ((y - mean) * lax.rsqrt(var + LNX_EPS)).reshape(B, T, RW_WIDTH) * lnx_w.astype(F32) + lnx_b.astype(F32)
    bonus = jnp.sum(rh * kh * r_k.astype(F32), axis=-1, keepdims=True) * vh
    y = y + bonus.reshape(B, T, RW_WIDTH)
    out = (y * g.astype(F32)).astype(p_rw.dtype)
    return out, wkv_new.astype(p_rw.dtype), p_rw[:, -1, :]


def mla_expand(ckv, krope, w_ukv, k_norm_g):
    B, S, _ = ckv.shape
    kv = (ckv @ w_ukv).reshape(B, S, MLA_H, QK_NOPE + V_HD)
    k_nope, v = kv[..., :QK_NOPE], kv[..., QK_NOPE:]
    k = jnp.concatenate([k_nope, jnp.broadcast_to(krope[:, :, None, :], (B, S, MLA_H, QK_ROPE))], axis=-1)
    return qk_norm(k, k_norm_g), v


def attention_prompt(q, k, v):
    B, T, H, _ = q.shape
    nblk = T // Q_BLOCK
    qb = q.reshape(B, nblk, Q_BLOCK, H, QK_DIM).transpose(1, 0, 2, 3, 4)
    key_chunk = jnp.arange(T) // CHUNK

    def one_block(args):
        qi, i = args
        q_chunk = (i * Q_BLOCK + jnp.arange(Q_BLOCK)) // CHUNK
        s = jnp.einsum('bqhd,bkhd->bhqk', qi, k).astype(F32) * QK_SCALE
        s = jnp.where(key_chunk[None, :] <= q_chunk[:, None], s, -jnp.inf)
        p = jax.nn.softmax(s, axis=-1).astype(v.dtype)
        return jnp.einsum('bhqk,bkhd->bqhd', p, v)

    out = lax.map(one_block, (qb, jnp.arange(nblk)))
    return out.transpose(1, 0, 2, 3, 4).reshape(B, T, H * V_HD)


def attention_sample(q, k, v):
    B, T, H, _ = q.shape
    s = jnp.einsum('bqhd,bkhd->bhqk', q, k).astype(F32) * QK_SCALE
    p = jax.nn.softmax(s, axis=-1).astype(v.dtype)
    return jnp.einsum('bhqk,bkhd->bqhd', p, v).reshape(B, T, H * V_HD)


def trunk_layer(x, cos, sin, shift_prev, wkv_prev, ckv_past, krope_past,
                norm_attn_g, w_in, rw_mu, rw_w0, rw_w2, rw_a0, rw_a2, rw_g2, rw_k_k, rw_k_a, rw_r_k,
                rw_lnx_w, rw_lnx_b, mla_q_a_g, mla_w_uq, mla_kv_a_g, mla_w_ukv, mla_q_norm_g, mla_k_norm_g,
                w_out, norm_ffn_g, w_gate, w_up, w_down):
    B, T, _ = x.shape
    h = rmsnorm(x, norm_attn_g)
    p = h @ w_in
    p_rw = p[..., :RW_COLS]
    p_q = p[..., RW_COLS:RW_COLS + Q_LORA]
    p_kv = p[..., RW_COLS + Q_LORA:RW_COLS + Q_LORA + KV_LORA]
    p_kr = p[..., RW_COLS + Q_LORA + KV_LORA:]
    y_rw, wkv_new, shift_new = rwkv7_time_mix(p_rw, shift_prev, wkv_prev, rw_mu, rw_w0, rw_w2, rw_a0, rw_a2,
                                               rw_g2, rw_k_k, rw_k_a, rw_r_k, rw_lnx_w, rw_lnx_b)
    q = (rmsnorm(p_q, mla_q_a_g) @ mla_w_uq).reshape(B, T, MLA_H, QK_DIM)
    q = jnp.concatenate([q[..., :QK_NOPE], apply_rope(q[..., QK_NOPE:], cos, sin)], axis=-1)
    q = qk_norm(q, mla_q_norm_g)
    ckv_new = rmsnorm(p_kv, mla_kv_a_g)
    krope_new = apply_rope(p_kr, cos, sin)
    if ckv_past is None:
        k, v = mla_expand(ckv_new, krope_new, mla_w_ukv, mla_k_norm_g)
        y_mla = attention_prompt(q, k, v)
    else:
        k, v = mla_expand(jnp.concatenate([ckv_past, ckv_new], axis=1),
                          jnp.concatenate([krope_past, krope_new], axis=1), mla_w_ukv, mla_k_norm_g)
        y_mla = attention_sample(q, k, v)
    x = x + jnp.concatenate([y_rw, y_mla], axis=-1) @ w_out
    h2 = rmsnorm(x, norm_ffn_g)
    x = x + (jax.nn.silu(h2 @ w_gate) * (h2 @ w_up)) @ w_down
    return x, ckv_new, krope_new, wkv_new, shift_new


def setup_inputs(seed: int = 0) -> dict:
    key = jax.random.key(seed)
    ks = jax.random.split(key, 32)
    nrm = lambda k, shape, scale: scale * jax.random.normal(k, shape, F32)
    L = DEPTH
    return {
        "x_prompt": nrm(ks[0], (BATCH, SEQ, D_MODEL), 1.0),
        "x_sample": nrm(ks[1], (DEC_BATCH, DEC_SEQ, D_MODEL), 1.0),
        "cache_ckv": nrm(ks[2], (L, DEC_BATCH, PAST_LEN, KV_LORA), 1.0),
        "cache_krope": nrm(ks[3], (L, DEC_BATCH, PAST_LEN, QK_ROPE), 1.0),
        "state_wkv": nrm(ks[4], (L, DEC_BATCH, RW_H, RW_HD, RW_HD), 0.3),
        "state_shift": nrm(ks[5], (L, DEC_BATCH, RW_COLS), 1.0),
        "norm_attn_g": 1.0 + nrm(ks[6], (L, D_MODEL), 0.05),
        "w_in": nrm(ks[7], (L, D_MODEL, IN_COLS), D_MODEL ** -0.5),
        "rw_mu": jax.random.uniform(ks[8], (L, RW_COLS), F32),
        "rw_w0": jax.random.uniform(ks[9], (L, RW_WIDTH), F32, minval=-2.0, maxval=0.5),
        "rw_w2": nrm(ks[10], (L, W_LORA, RW_WIDTH), 0.5 * W_LORA ** -0.5),
        "rw_a0": nrm(ks[11], (L, RW_WIDTH), 0.1),
        "rw_a2": nrm(ks[12], (L, A_LORA, RW_WIDTH), 0.5 * A_LORA ** -0.5),
        "rw_g2": nrm(ks[13], (L, G_LORA, RW_WIDTH), G_LORA ** -0.5),
        "rw_k_k": 0.85 + nrm(ks[14], (L, RW_WIDTH), 0.1),
        "rw_k_a": 1.0 + nrm(ks[15], (L, RW_WIDTH), 0.1),
        "rw_r_k": nrm(ks[16], (L, RW_H, RW_HD), 0.1),
        "rw_lnx_w": 1.0 + nrm(ks[17], (L, RW_WIDTH), 0.05),
        "rw_lnx_b": nrm(ks[18], (L, RW_WIDTH), 0.01),
        "mla_q_a_g": 1.0 + nrm(ks[19], (L, Q_LORA), 0.05),
        "mla_w_uq": nrm(ks[20], (L, Q_LORA, MLA_H * QK_DIM), Q_LORA ** -0.5),
        "mla_kv_a_g": 1.0 + nrm(ks[21], (L, KV_LORA), 0.05),
        "mla_w_ukv": nrm(ks[22], (L, KV_LORA, MLA_H * (QK_NOPE + V_HD)), KV_LORA ** -0.5),
        "mla_q_norm_g": 1.0 + nrm(ks[23], (L, QK_NOPE + QK_ROPE // 2), 0.05),
        "mla_k_norm_g": 1.0 + nrm(ks[24], (L, QK_NOPE + QK_ROPE // 2), 0.05),
        "w_out": nrm(ks[25], (L, MIX, D_MODEL), MIX ** -0.5),
        "norm_ffn_g": 1.0 + nrm(ks[26], (L, D_MODEL), 0.05),
        "w_gate": nrm(ks[27], (L, D_MODEL, D_FF), D_MODEL ** -0.5),
        "w_up": nrm(ks[28], (L, D_MODEL, D_FF), D_MODEL ** -0.5),
        "w_down": nrm(ks[29], (L, D_FF, D_MODEL), D_FF ** -0.5),
    }


def reference(x_prompt, x_sample, cache_ckv, cache_krope, state_wkv, state_shift,
              norm_attn_g, w_in, rw_mu, rw_w0, rw_w2, rw_a0, rw_a2, rw_g2, rw_k_k, rw_k_a, rw_r_k,
              rw_lnx_w, rw_lnx_b, mla_q_a_g, mla_w_uq, mla_kv_a_g, mla_w_ukv, mla_q_norm_g, mla_k_norm_g,
              w_out, norm_ffn_g, w_gate, w_up, w_down):
    Bp, Tp, _ = x_prompt.shape
    Ts = x_sample.shape[1]
    past = cache_ckv.shape[2]
    cos_p, sin_p = rope_cos_sin(jnp.arange(Tp))
    cos_s, sin_s = rope_cos_sin(past + jnp.arange(Ts))
    zero_shift = jnp.zeros((Bp, RW_COLS), x_prompt.dtype)
    zero_wkv = jnp.zeros((Bp, RW_H, RW_HD, RW_HD), x_prompt.dtype)
    xp, xs = x_prompt, x_sample
    p_ckv, p_kr, p_wkv, p_sh = [], [], [], []
    s_ckv, s_kr, s_wkv, s_sh = [], [], [], []
    for l in range(DEPTH):
        w = (norm_attn_g[l], w_in[l], rw_mu[l], rw_w0[l], rw_w2[l], rw_a0[l], rw_a2[l], rw_g2[l],
             rw_k_k[l], rw_k_a[l], rw_r_k[l], rw_lnx_w[l], rw_lnx_b[l], mla_q_a_g[l], mla_w_uq[l],
             mla_kv_a_g[l], mla_w_ukv[l], mla_q_norm_g[l], mla_k_norm_g[l], w_out[l],
             norm_ffn_g[l], w_gate[l], w_up[l], w_down[l])
        xp, ckv, kr, wkv, sh = trunk_layer(xp, cos_p, sin_p, zero_shift, zero_wkv, None, None, *w)
        p_ckv.append(ckv); p_kr.append(kr); p_wkv.append(wkv); p_sh.append(sh)
        xs, ckv, kr, wkv, sh = trunk_layer(xs, cos_s, sin_s, state_shift[l], state_wkv[l],
                                           cache_ckv[l], cache_krope[l], *w)
        s_ckv.append(ckv); s_kr.append(kr); s_wkv.append(wkv); s_sh.append(sh)
    return (xp, xs,
            jnp.stack(p_ckv), jnp.stack(p_kr), jnp.stack(p_wkv), jnp.stack(p_sh),
            jnp.stack(s_ckv), jnp.stack(s_kr), jnp.stack(s_wkv), jnp.stack(s_sh))
```

```python
import functools

import jax
import jax.numpy as jnp
from jax import lax
from jax.experimental import pallas as pl
from jax.experimental.pallas import tpu as pltpu

F32 = jnp.float32
BF16 = jnp.bfloat16

CHUNK = 64
RW_HD = 64
QK_NOPE = 128
QK_ROPE = 64
QK_DIM = QK_NOPE + QK_ROPE
V_HD = 128
ROPE_THETA = 10000.0
NORM_EPS = 1e-6
LNX_EPS = 64e-5
QK_SCALE = QK_DIM ** -0.5

LANE = 128
VMEM_LIMIT = 56 * 1024 * 1024

NN = (((1,), (0,)), ((), ()))
NT = (((1,), (1,)), ((), ()))
TN = (((0,), (0,)), ((), ()))


def _tile(n, pref, quantum):
    if n <= pref:
        return n
    t = (pref // quantum) * quantum
    while t >= quantum:
        if n % t == 0:
            return t
        t -= quantum
    raise ValueError(f"no tile for {n} with quantum {quantum}")


def _params(*sem):
    return pltpu.CompilerParams(dimension_semantics=sem, vmem_limit_bytes=VMEM_LIMIT)


def _rmsnorm_kernel(x_ref, g_ref, o_ref):
    x = x_ref[...]
    ms = jnp.mean(x * x, axis=-1, keepdims=True)
    o_ref[...] = (x * lax.rsqrt(ms + NORM_EPS) * g_ref[...]).astype(o_ref.dtype)


def rmsnorm_bf16(x, g):
    m, d = x.shape
    tm = _tile(m, 256, 8)
    return pl.pallas_call(
        _rmsnorm_kernel,
        grid=(m // tm,),
        in_specs=[pl.BlockSpec((tm, d), lambda i: (i, 0)),
                  pl.BlockSpec((1, d), lambda i: (0, 0))],
        out_specs=pl.BlockSpec((tm, d), lambda i: (i, 0)),
        out_shape=jax.ShapeDtypeStruct((m, d), BF16),
        compiler_params=_params("parallel"),
        name="rmsnorm",
    )(x, g.reshape(1, d))


def _mm_kernel(x_ref, w_ref, o_ref):
    o_ref[...] = jnp.dot(x_ref[...], w_ref[...], preferred_element_type=F32).astype(o_ref.dtype)


def _mm_res_kernel(x_ref, w_ref, r_ref, o_ref):
    o_ref[...] = r_ref[...] + jnp.dot(x_ref[...], w_ref[...], preferred_element_type=F32)


def _mm2_res_kernel(x1_ref, w1_ref, x2_ref, w2_ref, r_ref, o_ref):
    acc = jnp.dot(x1_ref[...], w1_ref[...], preferred_element_type=F32)
    acc = acc + jnp.dot(x2_ref[...], w2_ref[...], preferred_element_type=F32)
    o_ref[...] = r_ref[...] + acc


def _ffn_up_kernel(h_ref, wg_ref, wu_ref, o_ref):
    h = h_ref[...]
    a = jnp.dot(h, wg_ref[...], preferred_element_type=F32)
    b = jnp.dot(h, wu_ref[...], preferred_element_type=F32)
    o_ref[...] = (a * jax.nn.sigmoid(a) * b).astype(o_ref.dtype)


def matmul(x, w, *, tm_pref=1024, tn_pref=512, out_dtype=F32, res=None):
    m, k = x.shape
    n = w.shape[1]
    tm = _tile(m, tm_pref, 8)
    tn = _tile(n, tn_pref, LANE)
    in_specs = [pl.BlockSpec((tm, k), lambda i, j: (i, 0)),
                pl.BlockSpec((k, tn), lambda i, j: (0, j))]
    args = [x, w]
    body = _mm_kernel
    if res is not None:
        in_specs.append(pl.BlockSpec((tm, tn), lambda i, j: (i, j)))
        args.append(res)
        body = _mm_res_kernel
    return pl.pallas_call(
        body,
        grid=(m // tm, n // tn),
        in_specs=in_specs,
        out_specs=pl.BlockSpec((tm, tn), lambda i, j: (i, j)),
        out_shape=jax.ShapeDtypeStruct((m, n), out_dtype),
        compiler_params=_params("parallel", "arbitrary"),
        name="matmul",
    )(*args)


def out_proj(y1, w1, y2, w2, res):
    m, k1 = y1.shape
    k2 = y2.shape[1]
    n = w1.shape[1]
    tm = _tile(m, 1024, 8)
    tn = _tile(n, 512, LANE)
    return pl.pallas_call(
        _mm2_res_kernel,
        grid=(m // tm, n // tn),
        in_specs=[pl.BlockSpec((tm, k1), lambda i, j: (i, 0)),
                  pl.BlockSpec((k1, tn), lambda i, j: (0, j)),
                  pl.BlockSpec((tm, k2), lambda i, j: (i, 0)),
                  pl.BlockSpec((k2, tn), lambda i, j: (0, j)),
                  pl.BlockSpec((tm, tn), lambda i, j: (i, j))],
        out_specs=pl.BlockSpec((tm, tn), lambda i, j: (i, j)),
        out_shape=jax.ShapeDtypeStruct((m, n), F32),
        compiler_params=_params("parallel", "arbitrary"),
        name="out_proj",
    )(y1, w1, y2, w2, res)


def ffn_up(h, wg, wu):
    m, k = h.shape
    n = wg.shape[1]
    tm = _tile(m, 1024, 8)
    tn = _tile(n, 256, LANE)
    return pl.pallas_call(
        _ffn_up_kernel,
        grid=(m // tm, n // tn),
        in_specs=[pl.BlockSpec((tm, k), lambda i, j: (i, 0)),
                  pl.BlockSpec((k, tn), lambda i, j: (0, j)),
                  pl.BlockSpec((k, tn), lambda i, j: (0, j))],
        out_specs=pl.BlockSpec((tm, tn), lambda i, j: (i, j)),
        out_shape=jax.ShapeDtypeStruct((m, n), BF16),
        compiler_params=_params("parallel", "arbitrary"),
        name="ffn_up",
    )(h, wg, wu)


def _rw_prep_kernel(pr_ref, pk_ref, pv_ref, pl_ref, br_ref, bk_ref, bv_ref, bl_ref,
                    mur_ref, muk_ref, muv_ref, mul_ref, w0_ref, a0_ref, kk_ref, ka_ref,
                    w2_ref, a2_ref, g2_ref,
                    r_out, lw_out, k_out, v_out, kk_out, a_out, g_out):
    tm = pr_ref.shape[0]
    first = lax.broadcasted_iota(jnp.int32, (CHUNK, 1), 0) == 0

    def shifted(p_ref, b_ref, mu_ref):
        mu = mu_ref[...]
        outs = []
        for s in range(tm // CHUNK):
            sub = p_ref[CHUNK * s:CHUNK * (s + 1), :]
            prev = jnp.where(first, b_ref[s:s + 1, :], pltpu.roll(sub, 1, axis=0))
            outs.append(sub + (prev - sub) * mu)
        return outs[0] if len(outs) == 1 else jnp.concatenate(outs, axis=0)

    xr = shifted(pr_ref, br_ref, mur_ref)
    xk = shifted(pk_ref, bk_ref, muk_ref)
    xv = shifted(pv_ref, bv_ref, muv_ref)
    xl = shifted(pl_ref, bl_ref, mul_ref)

    dw = jnp.dot(jnp.tanh(xl).astype(BF16), w2_ref[...], preferred_element_type=F32)
    da = jnp.dot(xl.astype(BF16), a2_ref[...], preferred_element_type=F32)
    dg = jnp.dot(jax.nn.sigmoid(xl).astype(BF16), g2_ref[...], preferred_element_type=F32)

    w_log = -jax.nn.softplus(-(w0_ref[...] + dw)) - 0.5
    a = jax.nn.sigmoid(a0_ref[...] + da)
    r_out[...] = xr
    lw_out[...] = -jnp.exp(w_log)
    k_out[...] = xk * (1.0 + (a - 1.0) * ka_ref[...])
    v_out[...] = xv
    kk_out[...] = xk * kk_ref[...]
    a_out[...] = a
    g_out[...] = dg


def rw_prep(p_rw, bnd, mu, w0, a0, k_k, k_a, w2p, a2p, g2p, rw_w, lora_w):
    m = p_rw.shape[0]
    tm = _tile(m, 512, CHUNK)
    tc = _tile(rw_w, 512, LANE)
    ncb = rw_w // tc
    lora_blk = (3 * rw_w) // lora_w
    nb = tm // CHUNK

    def col(off):
        return lambda i, j: (i, off + j)

    p_specs = [pl.BlockSpec((tm, tc), col(0)), pl.BlockSpec((tm, tc), col(ncb)),
               pl.BlockSpec((tm, tc), col(2 * ncb)),
               pl.BlockSpec((tm, lora_w), lambda i, j: (i, lora_blk))]
    b_specs = [pl.BlockSpec((nb, tc), col(0)), pl.BlockSpec((nb, tc), col(ncb)),
               pl.BlockSpec((nb, tc), col(2 * ncb)),
               pl.BlockSpec((nb, lora_w), lambda i, j: (i, lora_blk))]
    mu_specs = [pl.BlockSpec((1, tc), lambda i, j: (0, j)),
                pl.BlockSpec((1, tc), lambda i, j: (0, ncb + j)),
                pl.BlockSpec((1, tc), lambda i, j: (0, 2 * ncb + j)),
                pl.BlockSpec((1, lora_w), lambda i, j: (0, lora_blk))]
    vec = pl.BlockSpec((1, tc), lambda i, j: (0, j))
    wspec = pl.BlockSpec((lora_w, tc), lambda i, j: (0, j))
    out_spec = pl.BlockSpec((tm, tc), lambda i, j: (i, j))
    out_shape = jax.ShapeDtypeStruct((m, rw_w), F32)
    return pl.pallas_call(
        _rw_prep_kernel,
        grid=(m // tm, ncb),
        in_specs=p_specs + b_specs + mu_specs + [vec, vec, vec, vec, wspec, wspec, wspec],
        out_specs=[out_spec] * 7,
        out_shape=[out_shape] * 7,
        compiler_params=_params("parallel", "arbitrary"),
        name="rw_prep",
    )(p_rw, p_rw, p_rw, p_rw, bnd, bnd, bnd, bnd, mu, mu, mu, mu, w0, a0, k_k, k_a, w2p, a2p, g2p)


def _split2(x):
    hi = x.astype(BF16)
    lo = (x - hi.astype(F32)).astype(BF16)
    return hi, lo


def _dot3(a, b, dims=NN):
    ah, al = _split2(a)
    bh, bl = _split2(b)
    d = functools.partial(lax.dot_general, dimension_numbers=dims, preferred_element_type=F32)
    return d(ah, bh) + (d(ah, bl) + d(al, bh))


def _cumsum_rows(tri, x):
    h = x.astype(BF16)
    r1 = x - h.astype(F32)
    m = r1.astype(BF16)
    lo = (r1 - m.astype(F32)).astype(BF16)
    d = functools.partial(jnp.dot, preferred_element_type=F32)
    return d(tri, h) + (d(tri, m) + d(tri, lo))


def _wkv_chunk_terms(r, lw, k, v, kkr, a, tri, strict, incl, eye):
    n = r.shape[0]
    ss = jnp.sum(kkr * kkr, axis=-1, keepdims=True)
    kk = kkr * lax.rsqrt(jnp.maximum(ss, 1e-24))
    b = kk * a
    cl = _cumsum_rows(tri, lw)
    c_tot = cl[n - 1:n, :]
    kd = kk * jnp.exp(cl - lw)
    rd = r * jnp.exp(cl)
    einv = jnp.exp(-cl)
    ki = k * einv
    bi = b * einv
    e_end = jnp.exp(c_tot - cl)
    ki_end = k * e_end
    bi_end = b * e_end

    kr = jnp.concatenate([kd, rd], axis=0)
    gk = _dot3(kr, ki, NT)
    gb = _dot3(kr, bi, NT)
    zero = jnp.zeros((n, n), F32)
    a_k = jnp.where(strict, gk[:n], zero)
    b_k = jnp.where(incl, gk[n:], zero)
    a_b = jnp.where(strict, gb[:n], zero)
    b_b = jnp.where(incl, gb[n:], zero)

    nk = -a_b
    t = jnp.where(eye, 1.0, 0.0) + nk
    steps = max(n.bit_length() - 2, 0)
    for _ in range(steps):
        nk = _dot3(nk, nk)
        t = t + _dot3(t, nk)

    akv = _dot3(a_k, v)
    kd2 = _dot3(t, kd)
    v2 = _dot3(t, akv)
    bk = _dot3(bi_end, kd2, TN)
    n_c = _dot3(ki_end, v, TN) - _dot3(bi_end, v2, TN)
    m_c = jnp.where(eye, jnp.exp(c_tot), 0.0) - bk
    r_p = rd - _dot3(b_b, kd2)
    y_p = _dot3(b_k, v) - _dot3(b_b, v2)
    return m_c, n_c, r_p, y_p


def _wkv_kernel(r_ref, lw_ref, k_ref, v_ref, kk_ref, a_ref, g_ref, p0_ref, lnw_ref, lnb_ref, rk_ref,
                y_ref, p_ref, *, n_chunks):
    c_idx = pl.program_id(2)

    @pl.when(c_idx == 0)
    def _():
        p_ref[...] = p0_ref[...]

    n = CHUNK
    rows = lax.broadcasted_iota(jnp.int32, (n, n), 0)
    cols = lax.broadcasted_iota(jnp.int32, (n, n), 1)
    strict = cols < rows
    incl = cols <= rows
    eye = cols == rows
    tri = jnp.where(incl, 1.0, 0.0).astype(BF16)

    hd = RW_HD
    for h in range(2):
        lanes = slice(h * hd, (h + 1) * hd)
        terms = []
        for c in range(n_chunks):
            rs = slice(c * n, (c + 1) * n)
            terms.append(_wkv_chunk_terms(r_ref[rs, lanes], lw_ref[rs, lanes], k_ref[rs, lanes], v_ref[rs, lanes],
                                          kk_ref[rs, lanes], a_ref[rs, lanes], tri, strict, incl, eye))
        p = p_ref[0, h]
        lnw = lnw_ref[:, lanes]
        lnb = lnb_ref[:, lanes]
        rk = rk_ref[:, lanes]
        for c in range(n_chunks):
            rs = slice(c * n, (c + 1) * n)
            m_c, n_c, r_p, y_p = terms[c]
            y = _dot3(r_p, p) + y_p
            p = _dot3(m_c, p) + n_c
            mean = jnp.mean(y, axis=-1, keepdims=True)
            yc = y - mean
            var = jnp.mean(yc * yc, axis=-1, keepdims=True)
            yn = yc * lax.rsqrt(var + LNX_EPS) * lnw + lnb
            rr = r_ref[rs, lanes]
            vv = v_ref[rs, lanes]
            bonus = jnp.sum(rr * k_ref[rs, lanes] * rk, axis=-1, keepdims=True) * vv
            y_ref[rs, lanes] = ((yn + bonus) * g_ref[rs, lanes]).astype(y_ref.dtype)
        p_ref[0, h] = p


def wkv(r, lw, k, v, kk, a, g, p0, lnw, lnb, rk, *, row0, n_seq, t_seq, blk):
    width = r.shape[1]
    n_pairs = width // LANE
    n_blk = t_seq // blk
    base = row0 // blk
    tok = pl.BlockSpec((blk, LANE), lambda s, p, c: (base + s * n_blk + c, p))
    vec = pl.BlockSpec((1, LANE), lambda s, p, c: (0, p))
    st = pl.BlockSpec((1, 2, RW_HD, RW_HD), lambda s, p, c: (s, p, 0, 0))
    return pl.pallas_call(
        functools.partial(_wkv_kernel, n_chunks=blk // CHUNK),
        grid=(n_seq, n_pairs, n_blk),
        in_specs=[tok] * 7 + [st, vec, vec, vec],
        out_specs=[pl.BlockSpec((blk, LANE), lambda s, p, c: (s * n_blk + c, p)), st],
        out_shape=[jax.ShapeDtypeStruct((n_seq * t_seq, width), BF16),
                   jax.ShapeDtypeStruct(p0.shape, F32)],
        compiler_params=_params("parallel", "parallel", "arbitrary"),
        name="wkv",
    )(r, lw, k, v, kk, a, g, p0, lnw, lnb, rk)


def _rope(x, cos2, sin2):
    half = QK_ROPE // 2
    rot = jnp.concatenate([-x[:, half:], x[:, :half]], axis=1)
    return x * cos2 + rot * sin2


def _q_prep_kernel(pq_ref, g_ref, w_ref, cos_ref, sin_ref, gn_ref, gr_ref, o_ref, *, n_heads):
    x = pq_ref[...]
    ms = jnp.mean(x * x, axis=-1, keepdims=True)
    xn = (x * lax.rsqrt(ms + NORM_EPS) * g_ref[...]).astype(BF16)
    cos2 = cos_ref[...]
    sin2 = sin_ref[...]
    gn = gn_ref[...]
    gr = gr_ref[...]
    for h in range(n_heads):
        q = jnp.dot(xn, w_ref[h], preferred_element_type=F32)
        nope = q[:, :QK_NOPE]
        rp = _rope(q[:, QK_NOPE:], cos2, sin2)
        ss = jnp.sum(nope * nope, axis=-1, keepdims=True) + jnp.sum(rp * rp, axis=-1, keepdims=True)
        scale = lax.rsqrt(ss * (1.0 / QK_DIM) + NORM_EPS) * QK_SCALE
        o_ref[h] = jnp.concatenate([nope * gn * scale, rp * gr * scale], axis=1).astype(o_ref.dtype)


def q_prep(p_q, g_qa, w_uq_h, cos2, sin2, gn, gr):
    m, ql = p_q.shape
    n_heads = w_uq_h.shape[0]
    tm = _tile(m, 256, 16)
    return pl.pallas_call(
        functools.partial(_q_prep_kernel, n_heads=n_heads),
        grid=(m // tm,),
        in_specs=[pl.BlockSpec((tm, ql), lambda i: (i, 0)),
                  pl.BlockSpec((1, ql), lambda i: (0, 0)),
                  pl.BlockSpec((n_heads, ql, QK_DIM), lambda i: (0, 0, 0)),
                  pl.BlockSpec((tm, QK_ROPE), lambda i: (i, 0)),
                  pl.BlockSpec((tm, QK_ROPE), lambda i: (i, 0)),
                  pl.BlockSpec((1, QK_NOPE), lambda i: (0, 0)),
                  pl.BlockSpec((1, QK_ROPE), lambda i: (0, 0))],
        out_specs=pl.BlockSpec((n_heads, tm, QK_DIM), lambda i: (0, i, 0)),
        out_shape=jax.ShapeDtypeStruct((n_heads, m, QK_DIM), BF16),
        compiler_params=_params("parallel"),
        name="q_prep",
    )(p_q, g_qa, w_uq_h, cos2, sin2, gn, gr)


def _kv_prep_kernel(p_ref, g_ref, cos_ref, sin_ref, ckv_ref, kr_ref, *, kv_lora):
    x = p_ref[:, :kv_lora]
    ms = jnp.mean(x * x, axis=-1, keepdims=True)
    ckv_ref[...] = x * lax.rsqrt(ms + NORM_EPS) * g_ref[...]
    kr_ref[...] = _rope(p_ref[:, kv_lora:kv_lora + QK_ROPE], cos_ref[...], sin_ref[...])


def kv_prep(p_kvr, g_kva, cos2, sin2, kv_lora):
    m, wd = p_kvr.shape
    tm = _tile(m, 512, 8)
    return pl.pallas_call(
        functools.partial(_kv_prep_kernel, kv_lora=kv_lora),
        grid=(m // tm,),
        in_specs=[pl.BlockSpec((tm, wd), lambda i: (i, 0)),
                  pl.BlockSpec((1, kv_lora), lambda i: (0, 0)),
                  pl.BlockSpec((tm, QK_ROPE), lambda i: (i, 0)),
                  pl.BlockSpec((tm, QK_ROPE), lambda i: (i, 0))],
        out_specs=[pl.BlockSpec((tm, kv_lora), lambda i: (i, 0)),
                   pl.BlockSpec((tm, QK_ROPE), lambda i: (i, 0))],
        out_shape=[jax.ShapeDtypeStruct((m, kv_lora), F32),
                   jax.ShapeDtypeStruct((m, QK_ROPE), F32)],
        compiler_params=_params("parallel"),
        name="kv_prep",
    )(p_kvr, g_kva, cos2, sin2)


def _kv_expand_kernel(c_ref, kr_ref, w_ref, gn_ref, gr_ref, k_ref, v_ref):
    kv = jnp.dot(c_ref[...].astype(BF16), w_ref[...], preferred_element_type=F32)
    kn = kv[:, :QK_NOPE]
    kr = kr_ref[...]
    ss = jnp.sum(kn * kn, axis=-1, keepdims=True) + jnp.sum(kr * kr, axis=-1, keepdims=True)
    rstd = lax.rsqrt(ss * (1.0 / QK_DIM) + NORM_EPS)
    k_ref[0] = jnp.concatenate([kn * gn_ref[...] * rstd, kr * gr_ref[...] * rstd], axis=1).astype(k_ref.dtype)
    v_ref[0] = kv[:, QK_NOPE:].astype(v_ref.dtype)


def kv_expand(ckv, krope, w_ukv, gn, gr, n_heads):
    rws, kv_lora = ckv.shape
    tr = _tile(rws, 1024, 16)
    hw = QK_NOPE + V_HD
    return pl.pallas_call(
        _kv_expand_kernel,
        grid=(rws // tr, n_heads),
        in_specs=[pl.BlockSpec((tr, kv_lora), lambda i, h: (i, 0)),
                  pl.BlockSpec((tr, QK_ROPE), lambda i, h: (i, 0)),
                  pl.BlockSpec((kv_lora, hw), lambda i, h: (0, h)),
                  pl.BlockSpec((1, QK_NOPE), lambda i, h: (0, 0)),
                  pl.BlockSpec((1, QK_ROPE), lambda i, h: (0, 0))],
        out_specs=[pl.BlockSpec((1, tr, QK_DIM), lambda i, h: (h, i, 0)),
                   pl.BlockSpec((1, tr, V_HD), lambda i, h: (h, i, 0))],
        out_shape=[jax.ShapeDtypeStruct((n_heads, rws, QK_DIM), BF16),
                   jax.ShapeDtypeStruct((n_heads, rws, V_HD), BF16)],
        compiler_params=_params("parallel", "arbitrary"),
        name="kv_expand",
    )(ckv, krope, w_ukv, gn, gr)


def _attn_prompt_kernel(q_ref, k_ref, v_ref, o_ref, *, blk):
    i = pl.program_id(2)
    q = q_ref[0]

    def step(kb, vb, carry, mask):
        m, l, acc = carry
        s = lax.dot_general(q, kb, NT, preferred_element_type=F32)
        if mask is not None:
            s = jnp.where(mask, s, -1e30)
        m_new = jnp.maximum(m, jnp.max(s, axis=-1, keepdims=True))
        p = jnp.exp(s - m_new)
        alpha = jnp.exp(m - m_new)
        l = alpha * l + jnp.sum(p, axis=-1, keepdims=True)
        acc = alpha * acc + jnp.dot(p.astype(BF16), vb, preferred_element_type=F32)
        return m_new, l, acc

    def body(j, carry):
        off = pl.multiple_of(j * blk, blk)
        return step(k_ref[0, pl.ds(off, blk), :], v_ref[0, pl.ds(off, blk), :], carry, None)

    init = (jnp.full((blk, 1), -1e30, F32), jnp.zeros((blk, 1), F32), jnp.zeros((blk, V_HD), F32))
    carry = lax.fori_loop(0, i, body, init)
    qc = lax.broadcasted_iota(jnp.int32, (blk, blk), 0) // CHUNK
    kc = lax.broadcasted_iota(jnp.int32, (blk, blk), 1) // CHUNK
    off = pl.multiple_of(i * blk, blk)
    m, l, acc = step(k_ref[0, pl.ds(off, blk), :], v_ref[0, pl.ds(off, blk), :], carry, kc <= qc)
    o_ref[...] = (acc / l).astype(o_ref.dtype)


def attn_prompt(q, k, v, *, n_seq, t_seq):
    n_heads = q.shape[0]
    blk = _tile(t_seq, 512, CHUNK)
    nq = t_seq // blk
    return pl.pallas_call(
        functools.partial(_attn_prompt_kernel, blk=blk),
        grid=(n_seq, n_heads, nq),
        in_specs=[pl.BlockSpec((1, blk, QK_DIM), lambda b, h, i: (h, b * nq + i, 0)),
                  pl.BlockSpec((1, t_seq, QK_DIM), lambda b, h, i: (h, b, 0)),
                  pl.BlockSpec((1, t_seq, V_HD), lambda b, h, i: (h, b, 0))],
        out_specs=pl.BlockSpec((blk, V_HD), lambda b, h, i: (b * nq + i, h)),
        out_shape=jax.ShapeDtypeStruct((n_seq * t_seq, n_heads * V_HD), BF16),
        compiler_params=_params("parallel", "parallel", "arbitrary"),
        name="attn_prompt",
    )(q, k, v)


def _attn_sample_kernel(q_ref, k_ref, v_ref, o_ref):
    s = lax.dot_general(q_ref[0], k_ref[0], NT, preferred_element_type=F32)
    m = jnp.max(s, axis=-1, keepdims=True)
    p = jnp.exp(s - m)
    l = jnp.sum(p, axis=-1, keepdims=True)
    acc = jnp.dot(p.astype(BF16), v_ref[0], preferred_element_type=F32)
    o_ref[...] = (acc / l).astype(o_ref.dtype)


def attn_sample(q, k, v, *, row0, n_seq, t_q, t_kv):
    n_heads = q.shape[0]
    base = row0 // t_q
    return pl.pallas_call(
        _attn_sample_kernel,
        grid=(n_seq, n_heads),
        in_specs=[pl.BlockSpec((1, t_q, QK_DIM), lambda b, h: (h, base + b, 0)),
                  pl.BlockSpec((1, t_kv, QK_DIM), lambda b, h: (h, b, 0)),
                  pl.BlockSpec((1, t_kv, V_HD), lambda b, h: (h, b, 0))],
        out_specs=pl.BlockSpec((t_q, V_HD), lambda b, h: (b, h)),
        out_shape=jax.ShapeDtypeStruct((n_seq * t_q, n_heads * V_HD), BF16),
        compiler_params=_params("parallel", "parallel"),
        name="attn_sample",
    )(q, k, v)


def _rope_tables(positions):
    half = QK_ROPE // 2
    inv_freq = ROPE_THETA ** (-jnp.arange(half, dtype=F32) / half)
    ang = positions.astype(F32)[:, None] * inv_freq[None, :]
    cos, sin = jnp.cos(ang), jnp.sin(ang)
    return jnp.concatenate([cos, cos], axis=1), jnp.concatenate([sin, sin], axis=1)


def _pad_cols(w, n):
    return jnp.pad(w, [(0, 0)] * (w.ndim - 1) + [(0, n - w.shape[-1])])


def kernel(x_prompt, x_sample, cache_ckv, cache_krope, state_wkv, state_shift, norm_attn_g, w_in, rw_mu, rw_w0, rw_w2, rw_a0, rw_a2, rw_g2, rw_k_k, rw_k_a, rw_r_k, rw_lnx_w, rw_lnx_b, mla_q_a_g, mla_w_uq, mla_kv_a_g, mla_w_ukv, mla_q_norm_g, mla_k_norm_g, w_out, norm_ffn_g, w_gate, w_up, w_down):
    bp, tp, d = x_prompt.shape
    bs, ts, _ = x_sample.shape
    depth = w_in.shape[0]
    past = cache_ckv.shape[2]
    kv_lora = cache_ckv.shape[3]
    rw_h = rw_r_k.shape[1]
    rw_w = rw_h * RW_HD
    w_lora, a_lora, g_lora = rw_w2.shape[1], rw_a2.shape[1], rw_g2.shape[1]
    lora = w_lora + a_lora + g_lora
    lora_w = -(-lora // LANE) * LANE
    rw_cols = 3 * rw_w + lora
    rw_pad = 3 * rw_w + lora_w
    q_lora = mla_w_uq.shape[1]
    mla_h = mla_w_uq.shape[2] // QK_DIM
    assert (3 * rw_w) % lora_w == 0 and tp % CHUNK == 0 and ts == CHUNK
    mp, ms = bp * tp, bs * ts
    m = mp + ms
    kvr_w = -(-(kv_lora + QK_ROPE) // LANE) * LANE

    w_rw = _pad_cols(w_in[:, :, :rw_cols], rw_pad).astype(BF16)
    w_q = w_in[:, :, rw_cols:rw_cols + q_lora].astype(BF16)
    w_kvr = _pad_cols(w_in[:, :, rw_cols + q_lora:], kvr_w).astype(BF16)
    mu_p = _pad_cols(rw_mu, rw_pad).reshape(depth, 1, rw_pad)
    zl = lambda w, r0: jnp.zeros((depth, lora_w, rw_w), F32).at[:, r0:r0 + w.shape[1]].set(w).astype(BF16)
    w2p, a2p, g2p = zl(rw_w2, 0), zl(rw_a2, w_lora), zl(rw_g2, w_lora + a_lora)
    vec = lambda t: t.reshape(depth, 1, -1)
    w_uq_h = mla_w_uq.reshape(depth, q_lora, mla_h, QK_DIM).transpose(0, 2, 1, 3).astype(BF16)
    w_ukv = mla_w_ukv.astype(BF16)
    w_o1 = w_out[:, :rw_w].astype(BF16)
    w_o2 = w_out[:, rw_w:].astype(BF16)
    w_g, w_u, w_d = w_gate.astype(BF16), w_up.astype(BF16), w_down.astype(BF16)
    qn_n, qn_r = mla_q_norm_g[:, None, :QK_NOPE], jnp.tile(mla_q_norm_g[:, None, QK_NOPE:], (1, 1, 2))
    kn_n, kn_r = mla_k_norm_g[:, None, :QK_NOPE], jnp.tile(mla_k_norm_g[:, None, QK_NOPE:], (1, 1, 2))

    pos = jnp.concatenate([jnp.tile(jnp.arange(tp), bp), jnp.tile(past + jnp.arange(ts), bs)])
    cos2, sin2 = _rope_tables(pos)

    x = jnp.concatenate([x_prompt.reshape(mp, d), x_sample.reshape(ms, d)], axis=0)
    zero_shift = jnp.zeros((bp, rw_pad), F32)
    zero_state = jnp.zeros((bp, rw_h, RW_HD, RW_HD), F32)
    nbp = tp // CHUNK
    outs = [[] for _ in range(8)]

    for l in range(depth):
        h = rmsnorm_bf16(x, norm_attn_g[l])
        p_rw = matmul(h, w_rw[l])
        p_q = matmul(h, w_q[l], tn_pref=q_lora)
        p_kvr = matmul(h, w_kvr[l], tn_pref=kvr_w)

        last = p_rw[CHUNK - 1::CHUNK]
        lp = last[:mp // CHUNK].reshape(bp, nbp, rw_pad)
        bnd_p = jnp.concatenate([zero_shift[:, None], lp[:, :-1]], axis=1).reshape(bp * nbp, rw_pad)
        bnd = jnp.concatenate([bnd_p, _pad_cols(state_shift[l], rw_pad)], axis=0)
        r, lw, k, v, kk, a, g = rw_prep(p_rw, bnd, mu_p[l], vec(rw_w0)[l], vec(rw_a0)[l], vec(rw_k_k)[l],
                                         vec(rw_k_a)[l], w2p[l], a2p[l], g2p[l], rw_w, lora_w)
        lnw, lnb, rk = vec(rw_lnx_w)[l], vec(rw_lnx_b)[l], rw_r_k[l].reshape(1, rw_w)
        y_rw_p, pst_p = wkv(r, lw, k, v, kk, a, g, zero_state, lnw, lnb, rk,
                            row0=0, n_seq=bp, t_seq=tp, blk=_tile(tp, 256, CHUNK))
        y_rw_s, pst_s = wkv(r, lw, k, v, kk, a, g, jnp.swapaxes(state_wkv[l], -1, -2), lnw, lnb, rk,
                            row0=mp, n_seq=bs, t_seq=ts, blk=ts)
        y_rw = jnp.concatenate([y_rw_p, y_rw_s], axis=0)

        q = q_prep(p_q, vec(mla_q_a_g)[l], w_uq_h[l], cos2, sin2, qn_n[l], qn_r[l])
        ckv_new, kr_new = kv_prep(p_kvr, vec(mla_kv_a_g)[l], cos2, sin2, kv_lora)
        k_p, v_p = kv_expand(ckv_new[:mp], kr_new[:mp], w_ukv[l], kn_n[l], kn_r[l], mla_h)
        y_mla_p = attn_prompt(q, k_p, v_p, n_seq=bp, t_seq=tp)
        ckv_s = jnp.concatenate([cache_ckv[l], ckv_new[mp:].reshape(bs, ts, kv_lora)], axis=1)
        kr_s = jnp.concatenate([cache_krope[l], kr_new[mp:].reshape(bs, ts, QK_ROPE)], axis=1)
        t_kv = past + ts
        k_s, v_s = kv_expand(ckv_s.reshape(bs * t_kv, kv_lora), kr_s.reshape(bs * t_kv, QK_ROPE),
                             w_ukv[l], kn_n[l], kn_r[l], mla_h)
        y_mla_s = attn_sample(q, k_s, v_s, row0=mp, n_seq=bs, t_q=ts, t_kv=t_kv)
        y_mla = jnp.concatenate([y_mla_p, y_mla_s], axis=0)

        x = out_proj(y_rw, w_o1[l], y_mla, w_o2[l], x)

        h2 = rmsnorm_bf16(x, norm_ffn_g[l])
        u = ffn_up(h2, w_g[l], w_u[l])
        x = matmul(u, w_d[l], tm_pref=512, tn_pref=256, res=x)

        outs[0].append(ckv_new[:mp].reshape(bp, tp, kv_lora))
        outs[1].append(kr_new[:mp].reshape(bp, tp, QK_ROPE))
        outs[2].append(jnp.swapaxes(pst_p, -1, -2))
        outs[3].append(p_rw[tp - 1:mp:tp, :rw_cols])
        outs[4].append(ckv_new[mp:].reshape(bs, ts, kv_lora))
        outs[5].append(kr_new[mp:].reshape(bs, ts, QK_ROPE))
        outs[6].append(jnp.swapaxes(pst_s, -1, -2))
        outs[7].append(p_rw[mp + ts - 1::ts, :rw_cols])

    return (x[:mp].reshape(bp, tp, d), x[mp:].reshape(bs, ts, d)) + tuple(jnp.stack(o) for o in outs)
```

```python
import functools

import jax
import jax.numpy as jnp
from jax import lax
from jax.experimental import pallas as pl
from jax.experimental.pallas import tpu as pltpu

F32 = jnp.float32
BF16 = jnp.bfloat16

CHUNK = 64
RW_HD = 64
QK_NOPE = 128
QK_ROPE = 64
QK_DIM = QK_NOPE + QK_ROPE
V_HD = 128
ROPE_THETA = 10000.0
NORM_EPS = 1e-6
LNX_EPS = 64e-5
QK_SCALE = QK_DIM ** -0.5

LANE = 128
VMEM_LIMIT = 56 * 1024 * 1024

NN = (((1,), (0,)), ((), ()))
NT = (((1,), (1,)), ((), ()))
TN = (((0,), (0,)), ((), ()))


def _tile(n, pref, quantum):
    if n <= pref:
        return n
    t = (pref // quantum) * quantum
    while t >= quantum:
        if n % t == 0:
            return t
        t -= quantum
    raise ValueError(f"no tile for {n} with quantum {quantum}")


def _params(*sem):
    return pltpu.CompilerParams(dimension_semantics=sem, vmem_limit_bytes=VMEM_LIMIT)


def _rmsnorm_kernel(x_ref, g_ref, o_ref):
    x = x_ref[...]
    ms = jnp.mean(x * x, axis=-1, keepdims=True)
    o_ref[...] = (x * lax.rsqrt(ms + NORM_EPS) * g_ref[...]).astype(o_ref.dtype)


def rmsnorm_bf16(x, g):
    m, d = x.shape
    tm = _tile(m, 256, 8)
    return pl.pallas_call(
        _rmsnorm_kernel,
        grid=(m // tm,),
        in_specs=[pl.BlockSpec((tm, d), lambda i: (i, 0)),
                  pl.BlockSpec((1, d), lambda i: (0, 0))],
        out_specs=pl.BlockSpec((tm, d), lambda i: (i, 0)),
        out_shape=jax.ShapeDtypeStruct((m, d), BF16),
        compiler_params=_params("parallel"),
        name="rmsnorm",
    )(x, g.reshape(1, d))


def _mm_kernel(x_ref, w_ref, o_ref):
    o_ref[...] = jnp.dot(x_ref[...], w_ref[...], preferred_element_type=F32).astype(o_ref.dtype)


def _mm_res_kernel(x_ref, w_ref, r_ref, o_ref):
    o_ref[...] = r_ref[...] + jnp.dot(x_ref[...], w_ref[...], preferred_element_type=F32)


def _mm2_res_kernel(x1_ref, w1_ref, x2_ref, w2_ref, r_ref, o_ref):
    acc = jnp.dot(x1_ref[...], w1_ref[...], preferred_element_type=F32)
    acc = acc + jnp.dot(x2_ref[...], w2_ref[...], preferred_element_type=F32)
    o_ref[...] = r_ref[...] + acc


def _ffn_up_kernel(h_ref, wg_ref, wu_ref, o_ref):
    h = h_ref[...]
    a = jnp.dot(h, wg_ref[...], preferred_element_type=F32)
    b = jnp.dot(h, wu_ref[...], preferred_element_type=F32)
    o_ref[...] = (a * jax.nn.sigmoid(a) * b).astype(o_ref.dtype)


def matmul(x, w, *, tm_pref=1024, tn_pref=512, out_dtype=F32, res=None):
    m, k = x.shape
    n = w.shape[1]
    tm = _tile(m, tm_pref, 8)
    tn = _tile(n, tn_pref, LANE)
    in_specs = [pl.BlockSpec((tm, k), lambda i, j: (i, 0)),
                pl.BlockSpec((k, tn), lambda i, j: (0, j))]
    args = [x, w]
    body = _mm_kernel
    if res is not None:
        in_specs.append(pl.BlockSpec((tm, tn), lambda i, j: (i, j)))
        args.append(res)
        body = _mm_res_kernel
    return pl.pallas_call(
        body,
        grid=(m // tm, n // tn),
        in_specs=in_specs,
        out_specs=pl.BlockSpec((tm, tn), lambda i, j: (i, j)),
        out_shape=jax.ShapeDtypeStruct((m, n), out_dtype),
        compiler_params=_params("parallel", "arbitrary"),
        name="matmul",
    )(*args)


def out_proj(y1, w1, y2, w2, res):
    m, k1 = y1.shape
    k2 = y2.shape[1]
    n = w1.shape[1]
    tm = _tile(m, 1024, 8)
    tn = _tile(n, 512, LANE)
    return pl.pallas_call(
        _mm2_res_kernel,
        grid=(m // tm, n // tn),
        in_specs=[pl.BlockSpec((tm, k1), lambda i, j: (i, 0)),
                  pl.BlockSpec((k1, tn), lambda i, j: (0, j)),
                  pl.BlockSpec((tm, k2), lambda i, j: (i, 0)),
                  pl.BlockSpec((k2, tn), lambda i, j: (0, j)),
                  pl.BlockSpec((tm, tn), lambda i, j: (i, j))],
        out_specs=pl.BlockSpec((tm, tn), lambda i, j: (i, j)),
        out_shape=jax.ShapeDtypeStruct((m, n), F32),
        compiler_params=_params("parallel", "arbitrary"),
        name="out_proj",
    )(y1, w1, y2, w2, res)


def ffn_up(h, wg, wu):
    m, k = h.shape
    n = wg.shape[1]
    tm = _tile(m, 1024, 8)
    tn = _tile(n, 256, LANE)
    return pl.pallas_call(
        _ffn_up_kernel,
        grid=(m // tm, n // tn),
        in_specs=[pl.BlockSpec((tm, k), lambda i, j: (i, 0)),
                  pl.BlockSpec((k, tn), lambda i, j: (0, j)),
                  pl.BlockSpec((k, tn), lambda i, j: (0, j))],
        out_specs=pl.BlockSpec((tm, tn), lambda i, j: (i, j)),
        out_shape=jax.ShapeDtypeStruct((m, n), BF16),
        compiler_params=_params("parallel", "arbitrary"),
        name="ffn_up",
    )(h, wg, wu)


def _rw_prep_kernel(pr_ref, pk_ref, pv_ref, pl_ref, br_ref, bk_ref, bv_ref, bl_ref,
                    mur_ref, muk_ref, muv_ref, mul_ref, w0_ref, a0_ref, kk_ref, ka_ref,
                    w2_ref, a2_ref, g2_ref,
                    r_out, lw_out, k_out, v_out, kk_out, a_out, g_out):
    tm = pr_ref.shape[0]
    first = lax.broadcasted_iota(jnp.int32, (CHUNK, 1), 0) == 0

    def shifted(p_ref, b_ref, mu_ref):
        mu = mu_ref[...]
        outs = []
        for s in range(tm // CHUNK):
            sub = p_ref[CHUNK * s:CHUNK * (s + 1), :]
            prev = jnp.where(first, b_ref[s:s + 1, :], pltpu.roll(sub, 1, axis=0))
            outs.append(sub + (prev - sub) * mu)
        return outs[0] if len(outs) == 1 else jnp.concatenate(outs, axis=0)

    xr = shifted(pr_ref, br_ref, mur_ref)
    xk = shifted(pk_ref, bk_ref, muk_ref)
    xv = shifted(pv_ref, bv_ref, muv_ref)
    xl = shifted(pl_ref, bl_ref, mul_ref)

    dw = jnp.dot(jnp.tanh(xl).astype(BF16), w2_ref[...], preferred_element_type=F32)
    da = jnp.dot(xl.astype(BF16), a2_ref[...], preferred_element_type=F32)
    dg = jnp.dot(jax.nn.sigmoid(xl).astype(BF16), g2_ref[...], preferred_element_type=F32)

    w_log = -jax.nn.softplus(-(w0_ref[...] + dw)) - 0.5
    a = jax.nn.sigmoid(a0_ref[...] + da)
    r_out[...] = xr
    lw_out[...] = -jnp.exp(w_log)
    k_out[...] = xk * (1.0 + (a - 1.0) * ka_ref[...])
    v_out[...] = xv
    kk_out[...] = xk * kk_ref[...]
    a_out[...] = a
    g_out[...] = dg


def rw_prep(p_rw, bnd, mu, w0, a0, k_k, k_a, w2p, a2p, g2p, rw_w, lora_w):
    m = p_rw.shape[0]
    tm = _tile(m, 512, CHUNK)
    tc = _tile(rw_w, 512, LANE)
    ncb = rw_w // tc
    lora_blk = (3 * rw_w) // lora_w
    nb = tm // CHUNK

    def col(off):
        return lambda i, j: (i, off + j)

    p_specs = [pl.BlockSpec((tm, tc), col(0)), pl.BlockSpec((tm, tc), col(ncb)),
               pl.BlockSpec((tm, tc), col(2 * ncb)),
               pl.BlockSpec((tm, lora_w), lambda i, j: (i, lora_blk))]
    b_specs = [pl.BlockSpec((nb, tc), col(0)), pl.BlockSpec((nb, tc), col(ncb)),
               pl.BlockSpec((nb, tc), col(2 * ncb)),
               pl.BlockSpec((nb, lora_w), lambda i, j: (i, lora_blk))]
    mu_specs = [pl.BlockSpec((1, tc), lambda i, j: (0, j)),
                pl.BlockSpec((1, tc), lambda i, j: (0, ncb + j)),
                pl.BlockSpec((1, tc), lambda i, j: (0, 2 * ncb + j)),
                pl.BlockSpec((1, lora_w), lambda i, j: (0, lora_blk))]
    vec = pl.BlockSpec((1, tc), lambda i, j: (0, j))
    wspec = pl.BlockSpec((lora_w, tc), lambda i, j: (0, j))
    out_spec = pl.BlockSpec((tm, tc), lambda i, j: (i, j))
    out_shape = jax.ShapeDtypeStruct((m, rw_w), F32)
    return pl.pallas_call(
        _rw_prep_kernel,
        grid=(m // tm, ncb),
        in_specs=p_specs + b_specs + mu_specs + [vec, vec, vec, vec, wspec, wspec, wspec],
        out_specs=[out_spec] * 7,
        out_shape=[out_shape] * 7,
        compiler_params=_params("parallel", "arbitrary"),
        name="rw_prep",
    )(p_rw, p_rw, p_rw, p_rw, bnd, bnd, bnd, bnd, mu, mu, mu, mu, w0, a0, k_k, k_a, w2p, a2p, g2p)


INV_BASE = 8

WKV_PASSES = dict(g=1, inv=1, apply=1, out=1, seq=1)


def _sp(x):
    hi = x.astype(BF16)
    return hi, (x - hi.astype(F32)).astype(BF16)


def _mm(a, b, dims, passes):
    d = functools.partial(lax.dot_general, dimension_numbers=dims, preferred_element_type=F32)
    out = d(a[0], b[0])
    if passes == 3:
        out = out + (d(a[0], b[1]) + d(a[1], b[0]))
    return out


def _head_sum(x, ones_bd):
    hi, lo = _sp(x)
    return (jnp.dot(hi, ones_bd, preferred_element_type=F32) + jnp.dot(lo, ones_bd, preferred_element_type=F32))


def _bd(x):
    lane = lax.broadcasted_iota(jnp.int32, x.shape, 1)
    return jnp.concatenate([jnp.where(lane < RW_HD, x, 0.0), jnp.where(lane >= RW_HD, x, 0.0)], axis=0)


def _wkv_terms(r, lw, k, v, kkr, a, nb, ps):
    n = CHUNK
    blk = nb * n
    rin = lax.broadcasted_iota(jnp.int32, (blk, LANE), 0) % n
    r2 = lax.broadcasted_iota(jnp.int32, (LANE, LANE), 0)
    c2 = lax.broadcasted_iota(jnp.int32, (LANE, LANE), 1)
    same = (r2 // RW_HD) == (c2 // RW_HD)
    strict = same & ((c2 % RW_HD) < (r2 % RW_HD))
    incl = same & ((c2 % RW_HD) <= (r2 % RW_HD))
    eye = r2 == c2
    ones_bd = jnp.where(same, 1.0, 0.0).astype(BF16)

    kk = kkr * lax.rsqrt(jnp.maximum(_head_sum(kkr * kkr, ones_bd), 1e-24))
    b = kk * a
    cl = lw
    s = 1
    while s < n:
        cl = cl + jnp.where(rin >= s, pltpu.roll(cl, s, axis=0), 0.0)
        s *= 2
    ct = jnp.broadcast_to(cl.reshape(nb, n, LANE)[:, n - 1:n, :], (nb, n, LANE)).reshape(blk, LANE)
    kd = kk * jnp.exp(cl - lw)
    rd = r * jnp.exp(cl)
    einv = jnp.exp(-cl)
    ki = k * einv
    bi = b * einv
    e_end = jnp.exp(ct - cl)
    ki_e = k * e_end
    bi_e = b * e_end
    ect = jnp.exp(ct)

    rng = range(nb)
    sl = lambda x, c: x[c * n:(c + 1) * n]
    kd_bd = [_bd(sl(kd, c)) for c in rng]
    rd_bd = [_bd(sl(rd, c)) for c in rng]
    v_sp = [_sp(_bd(sl(v, c))) for c in rng]
    kr_sp = [_sp(jnp.concatenate([kd_bd[c], rd_bd[c]], axis=0)) for c in rng]
    bi_sp = [_sp(_bd(sl(bi, c))) for c in rng]
    ki_sp = [_sp(_bd(sl(ki, c))) for c in rng]
    g1 = [_mm(kr_sp[c], bi_sp[c], NT, ps["g"]) for c in rng]
    g2 = [_mm(kr_sp[c], ki_sp[c], NT, ps["g"]) for c in rng]
    a_k = [_sp(jnp.where(strict, g2[c][:LANE], 0.0)) for c in rng]
    b_k = [_sp(jnp.where(incl, g2[c][LANE:], 0.0)) for c in rng]
    b_b = [_sp(jnp.where(incl, g1[c][LANE:], 0.0)) for c in rng]

    a_b = [jnp.where(strict, g1[c][:LANE], 0.0) for c in rng]
    blk_mask = lambda w: (r2 // w) == (c2 // w)
    n0 = [jnp.where(blk_mask(INV_BASE), -a_b[c], 0.0) for c in rng]
    t = [jnp.where(eye, 1.0, 0.0) + n0[c] for c in rng]
    nk = [_sp(x) for x in n0]
    s = 2
    while s < INV_BASE:
        nk = [_sp(_mm(nk[c], nk[c], NN, ps["inv"])) for c in rng]
        t = [t[c] + _mm(_sp(t[c]), nk[c], NN, ps["inv"]) for c in rng]
        s *= 2
    w = INV_BASE
    while w < n:
        off = blk_mask(2 * w) & jnp.logical_not(blk_mask(w))
        t_sp = [_sp(x) for x in t]
        lt = [_sp(_mm(_sp(jnp.where(off, a_b[c], 0.0)), t_sp[c], NN, ps["inv"])) for c in rng]
        t = [t[c] - _mm(t_sp[c], lt[c], NN, ps["inv"]) for c in rng]
        w *= 2
    t_sp = [_sp(x) for x in t]

    akv = [_mm(a_k[c], v_sp[c], NN, ps["apply"]) for c in rng]
    x_sp = [_sp(_mm(t_sp[c], _sp(jnp.concatenate([kd_bd[c], akv[c]], axis=1)), NN, ps["apply"])) for c in rng]
    bx = [_mm(_sp(_bd(sl(bi_e, c))), x_sp[c], TN, ps["out"]) for c in rng]
    kv = [_mm(_sp(_bd(sl(ki_e, c))), v_sp[c], TN, ps["out"]) for c in rng]
    ry = [_mm(b_b[c], x_sp[c], NN, ps["out"]) for c in rng]
    bkv = [_mm(b_k[c], v_sp[c], NN, ps["out"]) for c in rng]
    m_c = [jnp.where(eye, ect[c * n:c * n + 1, :], 0.0) - bx[c][:, :LANE] for c in rng]
    n_c = [kv[c] - bx[c][:, LANE:] for c in rng]
    r_p = [rd_bd[c] - ry[c][:, :LANE] for c in rng]
    y_p = [bkv[c] - ry[c][:, LANE:] for c in rng]
    return m_c, n_c, r_p, y_p, ones_bd


def _wkv_epilogue(y, r, k, v, g, lnw, lnb, rk, ones_bd):
    inv_n = 1.0 / RW_HD
    mean = _head_sum(y, ones_bd) * inv_n
    yc = y - mean
    var = _head_sum(yc * yc, ones_bd) * inv_n
    yn = yc * lax.rsqrt(var + LNX_EPS) * lnw + lnb
    bonus = _head_sum(r * k * rk, ones_bd) * v
    return (yn + bonus) * g


def _wkv_kernel(r_ref, lw_ref, k_ref, v_ref, kk_ref, a_ref, g_ref, p0_ref, lnw_ref, lnb_ref, rk_ref,
                y_ref, pt_ref, *scratch, nb, npair, carried, ps):
    if carried:
        p_scr, = scratch
        c_idx = pl.program_id(2)

        @pl.when(c_idx == 0)
        def _():
            for pp in range(npair):
                p_scr[pp] = _bd(p0_ref[0, pp])

    terms = []
    for pp in range(npair):
        ln = slice(pp * LANE, (pp + 1) * LANE)
        terms.append(_wkv_terms(r_ref[:, ln], lw_ref[:, ln], k_ref[:, ln], v_ref[:, ln], kk_ref[:, ln],
                                a_ref[:, ln], nb, ps))
    p = [p_scr[pp] for pp in range(npair)] if carried else None
    ys = [[] for _ in range(npair)]
    for c in range(nb):
        for pp in range(npair):
            m_c, n_c, r_p, y_p, _ = terms[pp]
            p_sp = _sp(p[pp] if carried else _bd(p0_ref[c, pp]))
            y_bd = _mm(_sp(r_p[c]), p_sp, NN, ps["seq"]) + y_p[c]
            p_new = _mm(_sp(m_c[c]), p_sp, NN, ps["seq"]) + n_c[c]
            ys[pp].append(y_bd[:RW_HD] + y_bd[RW_HD:])
            if carried:
                p[pp] = p_new
            else:
                pt_ref[c, pp] = p_new[:RW_HD] + p_new[RW_HD:]
    for pp in range(npair):
        ln = slice(pp * LANE, (pp + 1) * LANE)
        y = jnp.concatenate(ys[pp], axis=0) if nb > 1 else ys[pp][0]
        y_ref[:, ln] = _wkv_epilogue(y, r_ref[:, ln], k_ref[:, ln], v_ref[:, ln], g_ref[:, ln], lnw_ref[:, ln],
                                     lnb_ref[:, ln], rk_ref[:, ln], terms[pp][4]).astype(y_ref.dtype)
    if carried:
        for pp in range(npair):
            p_scr[pp] = p[pp]

        @pl.when(c_idx == pl.num_programs(2) - 1)
        def _():
            for pp in range(npair):
                pt_ref[0, pp] = p[pp][:RW_HD] + p[pp][RW_HD:]


def wkv_seq(r, lw, k, v, kk, a, g, pt0, lnw, lnb, rk, *, row0, n_seq, t_seq, nb_pref=16, npair_pref=2):
    width = r.shape[1]
    n_pairs = width // LANE
    npair = _tile(n_pairs, npair_pref, 1)
    blk = _tile(t_seq, nb_pref * CHUNK, CHUNK)
    n_blk = t_seq // blk
    base = row0 // blk
    tok = pl.BlockSpec((blk, npair * LANE), lambda s, p, c: (base + s * n_blk + c, p))
    vec = pl.BlockSpec((1, npair * LANE), lambda s, p, c: (0, p))
    st = pl.BlockSpec((1, npair, RW_HD, LANE), lambda s, p, c: (s, p, 0, 0))
    return pl.pallas_call(
        functools.partial(_wkv_kernel, nb=blk // CHUNK, npair=npair, carried=True, ps=WKV_PASSES),
        grid=(n_seq, n_pairs // npair, n_blk),
        in_specs=[tok] * 7 + [st, vec, vec, vec],
        out_specs=[pl.BlockSpec((blk, npair * LANE), lambda s, p, c: (s * n_blk + c, p)), st],
        out_shape=[jax.ShapeDtypeStruct((n_seq * t_seq, width), BF16),
                   jax.ShapeDtypeStruct(pt0.shape, F32)],
        scratch_shapes=[pltpu.VMEM((npair, LANE, LANE), F32)],
        compiler_params=_params("parallel", "parallel", "arbitrary"),
        name="wkv_seq",
    )(r, lw, k, v, kk, a, g, pt0, lnw, lnb, rk)


def wkv_par(r, lw, k, v, kk, a, g, pt0, lnw, lnb, rk, *, row0, n_seq, nb_pref=16, npair_pref=2):
    width = r.shape[1]
    n_pairs = width // LANE
    npair = _tile(n_pairs, npair_pref, 1)
    nb = _tile(n_seq, nb_pref, 1)
    blk = nb * CHUNK
    base = row0 // blk
    tok = pl.BlockSpec((blk, npair * LANE), lambda s, p: (base + s, p))
    vec = pl.BlockSpec((1, npair * LANE), lambda s, p: (0, p))
    st = pl.BlockSpec((nb, npair, RW_HD, LANE), lambda s, p: (s, p, 0, 0))
    return pl.pallas_call(
        functools.partial(_wkv_kernel, nb=nb, npair=npair, carried=False, ps=WKV_PASSES),
        grid=(n_seq // nb, n_pairs // npair),
        in_specs=[tok] * 7 + [st, vec, vec, vec],
        out_specs=[pl.BlockSpec((blk, npair * LANE), lambda s, p: (s, p)), st],
        out_shape=[jax.ShapeDtypeStruct((n_seq * CHUNK, width), BF16),
                   jax.ShapeDtypeStruct(pt0.shape, F32)],
        compiler_params=_params("parallel", "parallel"),
        name="wkv_par",
    )(r, lw, k, v, kk, a, g, pt0, lnw, lnb, rk)


def _rope(x, cos2, sin2):
    half = QK_ROPE // 2
    rot = jnp.concatenate([-x[:, half:], x[:, :half]], axis=1)
    return x * cos2 + rot * sin2


def _q_prep_kernel(pq_ref, g_ref, w_ref, cos_ref, sin_ref, gn_ref, gr_ref, o_ref, *, n_heads):
    x = pq_ref[...]
    ms = jnp.mean(x * x, axis=-1, keepdims=True)
    xn = (x * lax.rsqrt(ms + NORM_EPS) * g_ref[...]).astype(BF16)
    cos2 = cos_ref[...]
    sin2 = sin_ref[...]
    gn = gn_ref[...]
    gr = gr_ref[...]
    for h in range(n_heads):
        q = jnp.dot(xn, w_ref[h], preferred_element_type=F32)
        nope = q[:, :QK_NOPE]
        rp = _rope(q[:, QK_NOPE:], cos2, sin2)
        ss = jnp.sum(nope * nope, axis=-1, keepdims=True) + jnp.sum(rp * rp, axis=-1, keepdims=True)
        scale = lax.rsqrt(ss * (1.0 / QK_DIM) + NORM_EPS) * QK_SCALE
        o_ref[h] = jnp.concatenate([nope * gn * scale, rp * gr * scale], axis=1).astype(o_ref.dtype)


def q_prep(p_q, g_qa, w_uq_h, cos2, sin2, gn, gr):
    m, ql = p_q.shape
    n_heads = w_uq_h.shape[0]
    tm = _tile(m, 256, 16)
    return pl.pallas_call(
        functools.partial(_q_prep_kernel, n_heads=n_heads),
        grid=(m // tm,),
        in_specs=[pl.BlockSpec((tm, ql), lambda i: (i, 0)),
                  pl.BlockSpec((1, ql), lambda i: (0, 0)),
                  pl.BlockSpec((n_heads, ql, QK_DIM), lambda i: (0, 0, 0)),
                  pl.BlockSpec((tm, QK_ROPE), lambda i: (i, 0)),
                  pl.BlockSpec((tm, QK_ROPE), lambda i: (i, 0)),
                  pl.BlockSpec((1, QK_NOPE), lambda i: (0, 0)),
                  pl.BlockSpec((1, QK_ROPE), lambda i: (0, 0))],
        out_specs=pl.BlockSpec((n_heads, tm, QK_DIM), lambda i: (0, i, 0)),
        out_shape=jax.ShapeDtypeStruct((n_heads, m, QK_DIM), BF16),
        compiler_params=_params("parallel"),
        name="q_prep",
    )(p_q, g_qa, w_uq_h, cos2, sin2, gn, gr)


def _kv_prep_kernel(p_ref, g_ref, cos_ref, sin_ref, ckv_ref, kr_ref, *, kv_lora):
    x = p_ref[:, :kv_lora]
    ms = jnp.mean(x * x, axis=-1, keepdims=True)
    ckv_ref[...] = x * lax.rsqrt(ms + NORM_EPS) * g_ref[...]
    kr_ref[...] = _rope(p_ref[:, kv_lora:kv_lora + QK_ROPE], cos_ref[...], sin_ref[...])


def kv_prep(p_kvr, g_kva, cos2, sin2, kv_lora):
    m, wd = p_kvr.shape
    tm = _tile(m, 512, 8)
    return pl.pallas_call(
        functools.partial(_kv_prep_kernel, kv_lora=kv_lora),
        grid=(m // tm,),
        in_specs=[pl.BlockSpec((tm, wd), lambda i: (i, 0)),
                  pl.BlockSpec((1, kv_lora), lambda i: (0, 0)),
                  pl.BlockSpec((tm, QK_ROPE), lambda i: (i, 0)),
                  pl.BlockSpec((tm, QK_ROPE), lambda i: (i, 0))],
        out_specs=[pl.BlockSpec((tm, kv_lora), lambda i: (i, 0)),
                   pl.BlockSpec((tm, QK_ROPE), lambda i: (i, 0))],
        out_shape=[jax.ShapeDtypeStruct((m, kv_lora), F32),
                   jax.ShapeDtypeStruct((m, QK_ROPE), F32)],
        compiler_params=_params("parallel"),
        name="kv_prep",
    )(p_kvr, g_kva, cos2, sin2)


def _kv_expand_kernel(c_ref, kr_ref, w_ref, gn_ref, gr_ref, k_ref, v_ref, *, hb):
    hw = QK_NOPE + V_HD
    kv = jnp.dot(c_ref[...].astype(BF16), w_ref[...], preferred_element_type=F32)
    kr = kr_ref[...]
    ss_r = jnp.sum(kr * kr, axis=-1, keepdims=True)
    krg = kr * gr_ref[...]
    for h in range(hb):
        kn = kv[:, h * hw:h * hw + QK_NOPE]
        rstd = lax.rsqrt((jnp.sum(kn * kn, axis=-1, keepdims=True) + ss_r) * (1.0 / QK_DIM) + NORM_EPS)
        k_ref[h] = jnp.concatenate([kn * gn_ref[...] * rstd, krg * rstd], axis=1).astype(k_ref.dtype)
        v_ref[h] = kv[:, h * hw + QK_NOPE:(h + 1) * hw].astype(v_ref.dtype)


def kv_expand(ckv, krope, w_ukv, gn, gr, n_heads):
    rws, kv_lora = ckv.shape
    tr = _tile(rws, 1024, 16)
    hb = _tile(n_heads, 4, 1)
    hw = QK_NOPE + V_HD
    return pl.pallas_call(
        functools.partial(_kv_expand_kernel, hb=hb),
        grid=(rws // tr, n_heads // hb),
        in_specs=[pl.BlockSpec((tr, kv_lora), lambda i, h: (i, 0)),
                  pl.BlockSpec((tr, QK_ROPE), lambda i, h: (i, 0)),
                  pl.BlockSpec((kv_lora, hb * hw), lambda i, h: (0, h)),
                  pl.BlockSpec((1, QK_NOPE), lambda i, h: (0, 0)),
                  pl.BlockSpec((1, QK_ROPE), lambda i, h: (0, 0))],
        out_specs=[pl.BlockSpec((hb, tr, QK_DIM), lambda i, h: (h, i, 0)),
                   pl.BlockSpec((hb, tr, V_HD), lambda i, h: (h, i, 0))],
        out_shape=[jax.ShapeDtypeStruct((n_heads, rws, QK_DIM), BF16),
                   jax.ShapeDtypeStruct((n_heads, rws, V_HD), BF16)],
        compiler_params=_params("parallel", "arbitrary"),
        name="kv_expand",
    )(ckv, krope, w_ukv, gn, gr)


def _attn_prompt_kernel(q_ref, k_ref, v_ref, o_ref, *, blk):
    i = pl.program_id(2)
    q = q_ref[0]

    def step(kb, vb, carry, mask):
        m, l, acc = carry
        s = lax.dot_general(q, kb, NT, preferred_element_type=F32)
        if mask is not None:
            s = jnp.where(mask, s, -1e30)
        m_new = jnp.maximum(m, jnp.max(s, axis=-1, keepdims=True))
        p = jnp.exp(s - m_new)
        alpha = jnp.exp(m - m_new)
        l = alpha * l + jnp.sum(p, axis=-1, keepdims=True)
        acc = alpha * acc + jnp.dot(p.astype(BF16), vb, preferred_element_type=F32)
        return m_new, l, acc

    def body(j, carry):
        off = pl.multiple_of(j * blk, blk)
        return step(k_ref[0, pl.ds(off, blk), :], v_ref[0, pl.ds(off, blk), :], carry, None)

    init = (jnp.full((blk, 1), -1e30, F32), jnp.zeros((blk, 1), F32), jnp.zeros((blk, V_HD), F32))
    carry = lax.fori_loop(0, i, body, init)
    qc = lax.broadcasted_iota(jnp.int32, (blk, blk), 0) // CHUNK
    kc = lax.broadcasted_iota(jnp.int32, (blk, blk), 1) // CHUNK
    off = pl.multiple_of(i * blk, blk)
    m, l, acc = step(k_ref[0, pl.ds(off, blk), :], v_ref[0, pl.ds(off, blk), :], carry, kc <= qc)
    o_ref[...] = (acc / l).astype(o_ref.dtype)


def attn_prompt(q, k, v, *, n_seq, t_seq):
    n_heads = q.shape[0]
    blk = _tile(t_seq, 1024, CHUNK)
    nq = t_seq // blk
    return pl.pallas_call(
        functools.partial(_attn_prompt_kernel, blk=blk),
        grid=(n_seq, n_heads, nq),
        in_specs=[pl.BlockSpec((1, blk, QK_DIM), lambda b, h, i: (h, b * nq + i, 0)),
                  pl.BlockSpec((1, t_seq, QK_DIM), lambda b, h, i: (h, b, 0)),
                  pl.BlockSpec((1, t_seq, V_HD), lambda b, h, i: (h, b, 0))],
        out_specs=pl.BlockSpec((blk, V_HD), lambda b, h, i: (b * nq + i, h)),
        out_shape=jax.ShapeDtypeStruct((n_seq * t_seq, n_heads * V_HD), BF16),
        compiler_params=_params("parallel", "parallel", "arbitrary"),
        name="attn_prompt",
    )(q, k, v)


def _attn_sample_kernel(q_ref, k_ref, v_ref, o_ref):
    s = lax.dot_general(q_ref[0], k_ref[0], NT, preferred_element_type=F32)
    m = jnp.max(s, axis=-1, keepdims=True)
    p = jnp.exp(s - m)
    l = jnp.sum(p, axis=-1, keepdims=True)
    acc = jnp.dot(p.astype(BF16), v_ref[0], preferred_element_type=F32)
    o_ref[...] = (acc / l).astype(o_ref.dtype)


def attn_sample(q, k, v, *, row0, n_seq, t_q, t_kv):
    n_heads = q.shape[0]
    base = row0 // t_q
    return pl.pallas_call(
        _attn_sample_kernel,
        grid=(n_seq, n_heads),
        in_specs=[pl.BlockSpec((1, t_q, QK_DIM), lambda b, h: (h, base + b, 0)),
                  pl.BlockSpec((1, t_kv, QK_DIM), lambda b, h: (h, b, 0)),
                  pl.BlockSpec((1, t_kv, V_HD), lambda b, h: (h, b, 0))],
        out_specs=pl.BlockSpec((t_q, V_HD), lambda b, h: (b, h)),
        out_shape=jax.ShapeDtypeStruct((n_seq * t_q, n_heads * V_HD), BF16),
        compiler_params=_params("parallel", "parallel"),
        name="attn_sample",
    )(q, k, v)


def _rope_tables(positions):
    half = QK_ROPE // 2
    inv_freq = ROPE_THETA ** (-jnp.arange(half, dtype=F32) / half)
    ang = positions.astype(F32)[:, None] * inv_freq[None, :]
    cos, sin = jnp.cos(ang), jnp.sin(ang)
    return jnp.concatenate([cos, cos], axis=1), jnp.concatenate([sin, sin], axis=1)


def _pad_cols(w, n):
    return jnp.pad(w, [(0, 0)] * (w.ndim - 1) + [(0, n - w.shape[-1])])


def kernel(x_prompt, x_sample, cache_ckv, cache_krope, state_wkv, state_shift, norm_attn_g, w_in, rw_mu, rw_w0, rw_w2, rw_a0, rw_a2, rw_g2, rw_k_k, rw_k_a, rw_r_k, rw_lnx_w, rw_lnx_b, mla_q_a_g, mla_w_uq, mla_kv_a_g, mla_w_ukv, mla_q_norm_g, mla_k_norm_g, w_out, norm_ffn_g, w_gate, w_up, w_down):
    bp, tp, d = x_prompt.shape
    bs, ts, _ = x_sample.shape
    depth = w_in.shape[0]
    past = cache_ckv.shape[2]
    kv_lora = cache_ckv.shape[3]
    rw_h = rw_r_k.shape[1]
    rw_w = rw_h * RW_HD
    w_lora, a_lora, g_lora = rw_w2.shape[1], rw_a2.shape[1], rw_g2.shape[1]
    lora = w_lora + a_lora + g_lora
    lora_w = -(-lora // LANE) * LANE
    rw_cols = 3 * rw_w + lora
    rw_pad = 3 * rw_w + lora_w
    q_lora = mla_w_uq.shape[1]
    mla_h = mla_w_uq.shape[2] // QK_DIM
    assert (3 * rw_w) % lora_w == 0 and tp % CHUNK == 0 and ts == CHUNK
    mp, ms = bp * tp, bs * ts
    m = mp + ms
    kvr_w = -(-(kv_lora + QK_ROPE) // LANE) * LANE

    w_rw = _pad_cols(w_in[:, :, :rw_cols], rw_pad).astype(BF16)
    w_q = w_in[:, :, rw_cols:rw_cols + q_lora].astype(BF16)
    w_kvr = _pad_cols(w_in[:, :, rw_cols + q_lora:], kvr_w).astype(BF16)
    mu_p = _pad_cols(rw_mu, rw_pad).reshape(depth, 1, rw_pad)
    zl = lambda w, r0: jnp.zeros((depth, lora_w, rw_w), F32).at[:, r0:r0 + w.shape[1]].set(w).astype(BF16)
    w2p, a2p, g2p = zl(rw_w2, 0), zl(rw_a2, w_lora), zl(rw_g2, w_lora + a_lora)
    vec = lambda t: t.reshape(depth, 1, -1)
    w_uq_h = mla_w_uq.reshape(depth, q_lora, mla_h, QK_DIM).transpose(0, 2, 1, 3).astype(BF16)
    w_ukv = mla_w_ukv.astype(BF16)
    w_o1 = w_out[:, :rw_w].astype(BF16)
    w_o2 = w_out[:, rw_w:].astype(BF16)
    w_g, w_u, w_d = w_gate.astype(BF16), w_up.astype(BF16), w_down.astype(BF16)
    qn_n, qn_r = mla_q_norm_g[:, None, :QK_NOPE], jnp.tile(mla_q_norm_g[:, None, QK_NOPE:], (1, 1, 2))
    kn_n, kn_r = mla_k_norm_g[:, None, :QK_NOPE], jnp.tile(mla_k_norm_g[:, None, QK_NOPE:], (1, 1, 2))

    pos = jnp.concatenate([jnp.tile(jnp.arange(tp), bp), jnp.tile(past + jnp.arange(ts), bs)])
    cos2, sin2 = _rope_tables(pos)

    x = jnp.concatenate([x_prompt.reshape(mp, d), x_sample.reshape(ms, d)], axis=0)
    zero_shift = jnp.zeros((bp, rw_pad), F32)
    to_pairs = lambda s: (s.reshape(-1, rw_h // 2, 2, RW_HD, RW_HD).transpose(0, 1, 4, 2, 3)
                          .reshape(-1, rw_h // 2, RW_HD, 2 * RW_HD))
    from_pairs = lambda p: (p.reshape(-1, rw_h // 2, RW_HD, 2, RW_HD).transpose(0, 1, 3, 4, 2)
                            .reshape(-1, rw_h, RW_HD, RW_HD))
    zero_state = jnp.zeros((bp, rw_h // 2, RW_HD, 2 * RW_HD), F32)
    nbp = tp // CHUNK
    outs = [[] for _ in range(8)]

    for l in range(depth):
        h = rmsnorm_bf16(x, norm_attn_g[l])
        p_rw = matmul(h, w_rw[l])
        p_q = matmul(h, w_q[l], tn_pref=q_lora)
        p_kvr = matmul(h, w_kvr[l], tn_pref=kvr_w)

        last = p_rw[CHUNK - 1::CHUNK]
        lp = last[:mp // CHUNK].reshape(bp, nbp, rw_pad)
        bnd_p = jnp.concatenate([zero_shift[:, None], lp[:, :-1]], axis=1).reshape(bp * nbp, rw_pad)
        bnd = jnp.concatenate([bnd_p, _pad_cols(state_shift[l], rw_pad)], axis=0)
        r, lw, k, v, kk, a, g = rw_prep(p_rw, bnd, mu_p[l], vec(rw_w0)[l], vec(rw_a0)[l], vec(rw_k_k)[l],
                                         vec(rw_k_a)[l], w2p[l], a2p[l], g2p[l], rw_w, lora_w)
        lnw, lnb, rk = vec(rw_lnx_w)[l], vec(rw_lnx_b)[l], rw_r_k[l].reshape(1, rw_w)
        y_rw_p, pst_p = wkv_seq(r, lw, k, v, kk, a, g, zero_state, lnw, lnb, rk, row0=0, n_seq=bp, t_seq=tp)
        y_rw_s, pst_s = wkv_par(r, lw, k, v, kk, a, g, to_pairs(state_wkv[l]), lnw, lnb, rk, row0=mp, n_seq=bs)
        y_rw = jnp.concatenate([y_rw_p, y_rw_s], axis=0)

        q = q_prep(p_q, vec(mla_q_a_g)[l], w_uq_h[l], cos2, sin2, qn_n[l], qn_r[l])
        ckv_new, kr_new = kv_prep(p_kvr, vec(mla_kv_a_g)[l], cos2, sin2, kv_lora)
        k_p, v_p = kv_expand(ckv_new[:mp], kr_new[:mp], w_ukv[l], kn_n[l], kn_r[l], mla_h)
        y_mla_p = attn_prompt(q, k_p, v_p, n_seq=bp, t_seq=tp)
        ckv_s = jnp.concatenate([cache_ckv[l], ckv_new[mp:].reshape(bs, ts, kv_lora)], axis=1)
        kr_s = jnp.concatenate([cache_krope[l], kr_new[mp:].reshape(bs, ts, QK_ROPE)], axis=1)
        t_kv = past + ts
        k_s, v_s = kv_expand(ckv_s.reshape(bs * t_kv, kv_lora), kr_s.reshape(bs * t_kv, QK_ROPE),
                             w_ukv[l], kn_n[l], kn_r[l], mla_h)
        y_mla_s = attn_sample(q, k_s, v_s, row0=mp, n_seq=bs, t_q=ts, t_kv=t_kv)
        y_mla = jnp.concatenate([y_mla_p, y_mla_s], axis=0)

        x = out_proj(y_rw, w_o1[l], y_mla, w_o2[l], x)

        h2 = rmsnorm_bf16(x, norm_ffn_g[l])
        u = ffn_up(h2, w_g[l], w_u[l])
        x = matmul(u, w_d[l], tm_pref=512, tn_pref=256, res=x)

        outs[0].append(ckv_new[:mp].reshape(bp, tp, kv_lora))
        outs[1].append(kr_new[:mp].reshape(bp, tp, QK_ROPE))
        outs[2].append(from_pairs(pst_p))
        outs[3].append(p_rw[tp - 1:mp:tp, :rw_cols])
        outs[4].append(ckv_new[mp:].reshape(bs, ts, kv_lora))
        outs[5].append(kr_new[mp:].reshape(bs, ts, QK_ROPE))
        outs[6].append(from_pairs(pst_s))
        outs[7].append(p_rw[mp + ts - 1::ts, :rw_cols])

    return (x[:mp].reshape(bp, tp, d), x[mp:].reshape(bs, ts, d)) + tuple(jnp.stack(o) for o in outs)
```

```python
import functools
import math

import jax
import jax.numpy as jnp
from jax import lax
from jax.experimental import pallas as pl
from jax.experimental.pallas import tpu as pltpu

F32 = jnp.float32
BF16 = jnp.bfloat16

CHUNK = 64
RW_HD = 64
QK_NOPE = 128
QK_ROPE = 64
QK_DIM = QK_NOPE + QK_ROPE
V_HD = 128
ROPE_THETA = 10000.0
NORM_EPS = 1e-6
LNX_EPS = 64e-5
QK_SCALE = QK_DIM ** -0.5

LANE = 128
VMEM_LIMIT = 56 * 1024 * 1024

NN = (((1,), (0,)), ((), ()))
NT = (((1,), (1,)), ((), ()))
TN = (((0,), (0,)), ((), ()))


def _tile(n, pref, quantum):
    if n <= pref:
        return n
    t = (pref // quantum) * quantum
    while t >= quantum:
        if n % t == 0:
            return t
        t -= quantum
    raise ValueError(f"no tile for {n} with quantum {quantum}")


def _params(*sem):
    return pltpu.CompilerParams(dimension_semantics=sem, vmem_limit_bytes=VMEM_LIMIT)


def _rmsnorm_kernel(x_ref, g_ref, o_ref):
    x = x_ref[...]
    ms = jnp.mean(x * x, axis=-1, keepdims=True)
    o_ref[...] = (x * lax.rsqrt(ms + NORM_EPS) * g_ref[...]).astype(o_ref.dtype)


def rmsnorm_bf16(x, g):
    m, d = x.shape
    tm = _tile(m, 256, 8)
    return pl.pallas_call(
        _rmsnorm_kernel,
        grid=(m // tm,),
        in_specs=[pl.BlockSpec((tm, d), lambda i: (i, 0)),
                  pl.BlockSpec((1, d), lambda i: (0, 0))],
        out_specs=pl.BlockSpec((tm, d), lambda i: (i, 0)),
        out_shape=jax.ShapeDtypeStruct((m, d), BF16),
        compiler_params=_params("parallel"),
        name="rmsnorm",
    )(x, g.reshape(1, d))


def _mm_kernel(x_ref, w_ref, o_ref):
    o_ref[...] = jnp.dot(x_ref[...], w_ref[...], preferred_element_type=F32).astype(o_ref.dtype)


def _mm_res_kernel(x_ref, w_ref, r_ref, o_ref):
    o_ref[...] = r_ref[...] + jnp.dot(x_ref[...], w_ref[...], preferred_element_type=F32)


def _out_proj_kernel(y1p_ref, y1s_ref, y2p_ref, y2s_ref, w1_ref, w2_ref, r_ref, o_ref, *, n_p):
    def go(y1_ref, y2_ref):
        acc = jnp.dot(y1_ref[...], w1_ref[...], preferred_element_type=F32)
        acc = acc + jnp.dot(y2_ref[...], w2_ref[...], preferred_element_type=F32)
        o_ref[...] = r_ref[...] + acc

    @pl.when(pl.program_id(0) < n_p)
    def _():
        go(y1p_ref, y2p_ref)

    @pl.when(pl.program_id(0) >= n_p)
    def _():
        go(y1s_ref, y2s_ref)


def _ffn_up_kernel(h_ref, wg_ref, wu_ref, o_ref):
    h = h_ref[...]
    a = jnp.dot(h, wg_ref[...], preferred_element_type=F32)
    b = jnp.dot(h, wu_ref[...], preferred_element_type=F32)
    o_ref[...] = (a * jax.nn.sigmoid(a) * b).astype(o_ref.dtype)


def matmul(x, w, *, tm_pref=1024, tn_pref=512, out_dtype=F32, res=None):
    m, k = x.shape
    n = w.shape[1]
    tm = _tile(m, tm_pref, 8)
    tn = _tile(n, tn_pref, LANE)
    in_specs = [pl.BlockSpec((tm, k), lambda i, j: (i, 0)),
                pl.BlockSpec((k, tn), lambda i, j: (0, j))]
    args = [x, w]
    body = _mm_kernel
    if res is not None:
        in_specs.append(pl.BlockSpec((tm, tn), lambda i, j: (i, j)))
        args.append(res)
        body = _mm_res_kernel
    return pl.pallas_call(
        body,
        grid=(m // tm, n // tn),
        in_specs=in_specs,
        out_specs=pl.BlockSpec((tm, tn), lambda i, j: (i, j)),
        out_shape=jax.ShapeDtypeStruct((m, n), out_dtype),
        compiler_params=_params("parallel", "arbitrary"),
        name="matmul",
    )(*args)


def _mm_res_ksplit_kernel(x_ref, w_ref, r_ref, o_ref):
    part = jnp.dot(x_ref[...], w_ref[...], preferred_element_type=F32)

    @pl.when(pl.program_id(2) == 0)
    def _():
        o_ref[...] = r_ref[...] + part

    @pl.when(pl.program_id(2) > 0)
    def _():
        o_ref[...] += part


def matmul_res_ksplit(x, w, res, *, n_k=2, tm_pref=1024, tn_pref=512):
    m, k = x.shape
    n = w.shape[1]
    tm = _tile(m, tm_pref, 8)
    tn = _tile(n, tn_pref, LANE)
    assert k % (n_k * LANE) == 0
    tk = k // n_k
    return pl.pallas_call(
        _mm_res_ksplit_kernel,
        grid=(m // tm, n // tn, n_k),
        in_specs=[pl.BlockSpec((tm, tk), lambda i, j, kk: (i, kk)),
                  pl.BlockSpec((tk, tn), lambda i, j, kk: (kk, j)),
                  pl.BlockSpec((tm, tn), lambda i, j, kk: (i, j))],
        out_specs=pl.BlockSpec((tm, tn), lambda i, j, kk: (i, j)),
        out_shape=jax.ShapeDtypeStruct((m, n), F32),
        compiler_params=_params("parallel", "arbitrary", "arbitrary"),
        name="matmul_ksplit",
    )(x, w, res)


def _mm_last_kernel(x_ref, w_ref, o_ref, b_ref):
    o_ref[...] = jnp.dot(x_ref[...], w_ref[...], preferred_element_type=F32)
    for s in range(b_ref.shape[0]):
        b_ref[s:s + 1, :] = o_ref[CHUNK * s + CHUNK - 1:CHUNK * (s + 1), :]


def matmul_with_block_ends(x, w):
    m, k = x.shape
    n = w.shape[1]
    tm = _tile(m, 1024, 8 * CHUNK)
    tn = _tile(n, 512, LANE)
    return pl.pallas_call(
        _mm_last_kernel,
        grid=(m // tm, n // tn),
        in_specs=[pl.BlockSpec((tm, k), lambda i, j: (i, 0)),
                  pl.BlockSpec((k, tn), lambda i, j: (0, j))],
        out_specs=[pl.BlockSpec((tm, tn), lambda i, j: (i, j)),
                   pl.BlockSpec((tm // CHUNK, tn), lambda i, j: (i, j))],
        out_shape=[jax.ShapeDtypeStruct((m, n), F32), jax.ShapeDtypeStruct((m // CHUNK, n), F32)],
        compiler_params=_params("parallel", "arbitrary"),
        name="matmul_ends",
    )(x, w)


def out_proj(y1p, y1s, y2p, y2s, w1, w2, res):
    mp, k1 = y1p.shape
    ms, k2 = y2s.shape
    n = w1.shape[1]
    tm = _tile(math.gcd(mp, ms), 1024, 8)
    tn = _tile(n, 512, LANE)
    n_p = mp // tm
    row_p = lambda i, j: (jnp.minimum(i, n_p - 1), 0)
    row_s = lambda i, j: (jnp.maximum(i - n_p, 0), 0)
    return pl.pallas_call(
        functools.partial(_out_proj_kernel, n_p=n_p),
        grid=((mp + ms) // tm, n // tn),
        in_specs=[pl.BlockSpec((tm, k1), row_p), pl.BlockSpec((tm, k1), row_s),
                  pl.BlockSpec((tm, k2), row_p), pl.BlockSpec((tm, k2), row_s),
                  pl.BlockSpec((k1, tn), lambda i, j: (0, j)),
                  pl.BlockSpec((k2, tn), lambda i, j: (0, j)),
                  pl.BlockSpec((tm, tn), lambda i, j: (i, j))],
        out_specs=pl.BlockSpec((tm, tn), lambda i, j: (i, j)),
        out_shape=jax.ShapeDtypeStruct((mp + ms, n), F32),
        compiler_params=_params("parallel", "arbitrary"),
        name="out_proj",
    )(y1p, y1s, y2p, y2s, w1, w2, res)


def ffn_up(h, wg, wu):
    m, k = h.shape
    n = wg.shape[1]
    tm = _tile(m, 1024, 8)
    tn = _tile(n, 256, LANE)
    return pl.pallas_call(
        _ffn_up_kernel,
        grid=(m // tm, n // tn),
        in_specs=[pl.BlockSpec((tm, k), lambda i, j: (i, 0)),
                  pl.BlockSpec((k, tn), lambda i, j: (0, j)),
                  pl.BlockSpec((k, tn), lambda i, j: (0, j))],
        out_specs=pl.BlockSpec((tm, tn), lambda i, j: (i, j)),
        out_shape=jax.ShapeDtypeStruct((m, n), BF16),
        compiler_params=_params("parallel", "arbitrary"),
        name="ffn_up",
    )(h, wg, wu)


def _rw_prep_kernel(pr_ref, pk_ref, pv_ref, pl_ref, br_ref, bk_ref, bv_ref, bl_ref,
                    mur_ref, muk_ref, muv_ref, mul_ref, w0_ref, a0_ref, kk_ref, ka_ref,
                    w2_ref, a2_ref, g2_ref,
                    r_out, lw_out, k_out, v_out, kk_out, a_out, g_out):
    tm = pr_ref.shape[0]
    first = lax.broadcasted_iota(jnp.int32, (CHUNK, 1), 0) == 0

    def shifted(p_ref, b_ref, mu_ref):
        mu = mu_ref[...]
        outs = []
        for s in range(tm // CHUNK):
            sub = p_ref[CHUNK * s:CHUNK * (s + 1), :]
            prev = jnp.where(first, b_ref[s:s + 1, :], pltpu.roll(sub, 1, axis=0))
            outs.append(sub + (prev - sub) * mu)
        return outs[0] if len(outs) == 1 else jnp.concatenate(outs, axis=0)

    xr = shifted(pr_ref, br_ref, mur_ref)
    xk = shifted(pk_ref, bk_ref, muk_ref)
    xv = shifted(pv_ref, bv_ref, muv_ref)
    xl = shifted(pl_ref, bl_ref, mul_ref)

    dw = jnp.dot(jnp.tanh(xl).astype(BF16), w2_ref[...], preferred_element_type=F32)
    da = jnp.dot(xl.astype(BF16), a2_ref[...], preferred_element_type=F32)
    dg = jnp.dot(jax.nn.sigmoid(xl).astype(BF16), g2_ref[...], preferred_element_type=F32)

    w_log = -jax.nn.softplus(-(w0_ref[...] + dw)) - 0.5
    a = jax.nn.sigmoid(a0_ref[...] + da)
    r_out[...] = xr
    lw_out[...] = -jnp.exp(w_log)
    k_out[...] = xk * (1.0 + (a - 1.0) * ka_ref[...])
    v_out[...] = xv
    kk_out[...] = xk * kk_ref[...]
    a_out[...] = a
    g_out[...] = dg.astype(g_out.dtype)


def rw_prep(p_rw, bnd, mu, w0, a0, k_k, k_a, w2p, a2p, g2p, rw_w, lora_w):
    m = p_rw.shape[0]
    tm = _tile(m, 512, CHUNK)
    tc = _tile(rw_w, 512, LANE)
    ncb = rw_w // tc
    lora_blk = (3 * rw_w) // lora_w
    nb = tm // CHUNK

    def col(off):
        return lambda i, j: (i, off + j)

    p_specs = [pl.BlockSpec((tm, tc), col(0)), pl.BlockSpec((tm, tc), col(ncb)),
               pl.BlockSpec((tm, tc), col(2 * ncb)),
               pl.BlockSpec((tm, lora_w), lambda i, j: (i, lora_blk))]
    b_specs = [pl.BlockSpec((nb, tc), col(0)), pl.BlockSpec((nb, tc), col(ncb)),
               pl.BlockSpec((nb, tc), col(2 * ncb)),
               pl.BlockSpec((nb, lora_w), lambda i, j: (i, lora_blk))]
    mu_specs = [pl.BlockSpec((1, tc), lambda i, j: (0, j)),
                pl.BlockSpec((1, tc), lambda i, j: (0, ncb + j)),
                pl.BlockSpec((1, tc), lambda i, j: (0, 2 * ncb + j)),
                pl.BlockSpec((1, lora_w), lambda i, j: (0, lora_blk))]
    vec = pl.BlockSpec((1, tc), lambda i, j: (0, j))
    wspec = pl.BlockSpec((lora_w, tc), lambda i, j: (0, j))
    out_spec = pl.BlockSpec((tm, tc), lambda i, j: (i, j))
    out_shape = jax.ShapeDtypeStruct((m, rw_w), F32)
    return pl.pallas_call(
        _rw_prep_kernel,
        grid=(m // tm, ncb),
        in_specs=p_specs + b_specs + mu_specs + [vec, vec, vec, vec, wspec, wspec, wspec],
        out_specs=[out_spec] * 7,
        out_shape=[out_shape] * 6 + [jax.ShapeDtypeStruct((m, rw_w), BF16)],
        compiler_params=_params("parallel", "arbitrary"),
        name="rw_prep",
    )(p_rw, p_rw, p_rw, p_rw, bnd, bnd, bnd, bnd, mu, mu, mu, mu, w0, a0, k_k, k_a, w2p, a2p, g2p)


INV_BASE = 8

WKV_PASSES = dict(g=1, inv=1, apply=1, out=1, seq=1)


def _sp(x):
    hi = x.astype(BF16)
    return hi, (x - hi.astype(F32)).astype(BF16)


def _mm(a, b, dims, passes):
    d = functools.partial(lax.dot_general, dimension_numbers=dims, preferred_element_type=F32)
    out = d(a[0], b[0])
    if passes == 3:
        out = out + (d(a[0], b[1]) + d(a[1], b[0]))
    return out


def _head_sum(x, ones_bd):
    hi, lo = _sp(x)
    return (jnp.dot(hi, ones_bd, preferred_element_type=F32) + jnp.dot(lo, ones_bd, preferred_element_type=F32))


def _bd(x):
    lane = lax.broadcasted_iota(jnp.int32, x.shape, 1)
    return jnp.concatenate([jnp.where(lane < RW_HD, x, 0.0), jnp.where(lane >= RW_HD, x, 0.0)], axis=0)


def _wkv_terms(r, lw, k, v, kkr, a, nb, ps):
    n = CHUNK
    blk = nb * n
    rin = lax.broadcasted_iota(jnp.int32, (blk, LANE), 0) % n
    r2 = lax.broadcasted_iota(jnp.int32, (LANE, LANE), 0)
    c2 = lax.broadcasted_iota(jnp.int32, (LANE, LANE), 1)
    same = (r2 // RW_HD) == (c2 // RW_HD)
    strict = same & ((c2 % RW_HD) < (r2 % RW_HD))
    incl = same & ((c2 % RW_HD) <= (r2 % RW_HD))
    eye = r2 == c2
    ones_bd = jnp.where(same, 1.0, 0.0).astype(BF16)

    kk = kkr * lax.rsqrt(jnp.maximum(_head_sum(kkr * kkr, ones_bd), 1e-24))
    b = kk * a
    cl = lw
    s = 1
    while s < n:
        cl = cl + jnp.where(rin >= s, pltpu.roll(cl, s, axis=0), 0.0)
        s *= 2
    ct = jnp.broadcast_to(cl.reshape(nb, n, LANE)[:, n - 1:n, :], (nb, n, LANE)).reshape(blk, LANE)
    kd = kk * jnp.exp(cl - lw)
    rd = r * jnp.exp(cl)
    einv = jnp.exp(-cl)
    ki = k * einv
    bi = b * einv
    e_end = jnp.exp(ct - cl)
    ki_e = k * e_end
    bi_e = b * e_end
    ect = jnp.exp(ct)

    rng = range(nb)
    sl = lambda x, c: x[c * n:(c + 1) * n]
    kd_bd = [_bd(sl(kd, c)) for c in rng]
    rd_bd = [_bd(sl(rd, c)) for c in rng]
    v_sp = [_sp(_bd(sl(v, c))) for c in rng]
    kr_sp = [_sp(jnp.concatenate([kd_bd[c], rd_bd[c]], axis=0)) for c in rng]
    bi_sp = [_sp(_bd(sl(bi, c))) for c in rng]
    ki_sp = [_sp(_bd(sl(ki, c))) for c in rng]
    g1 = [_mm(kr_sp[c], bi_sp[c], NT, ps["g"]) for c in rng]
    g2 = [_mm(kr_sp[c], ki_sp[c], NT, ps["g"]) for c in rng]
    a_k = [_sp(jnp.where(strict, g2[c][:LANE], 0.0)) for c in rng]
    b_k = [_sp(jnp.where(incl, g2[c][LANE:], 0.0)) for c in rng]
    b_b = [_sp(jnp.where(incl, g1[c][LANE:], 0.0)) for c in rng]

    a_b = [jnp.where(strict, g1[c][:LANE], 0.0) for c in rng]
    blk_mask = lambda w: (r2 // w) == (c2 // w)
    n0 = [jnp.where(blk_mask(INV_BASE), -a_b[c], 0.0) for c in rng]
    t = [jnp.where(eye, 1.0, 0.0) + n0[c] for c in rng]
    nk = [_sp(x) for x in n0]
    s = 2
    while s < INV_BASE:
        nk = [_sp(_mm(nk[c], nk[c], NN, ps["inv"])) for c in rng]
        t = [t[c] + _mm(_sp(t[c]), nk[c], NN, ps["inv"]) for c in rng]
        s *= 2
    w = INV_BASE
    while w < n:
        off = blk_mask(2 * w) & jnp.logical_not(blk_mask(w))
        t_sp = [_sp(x) for x in t]
        lt = [_sp(_mm(_sp(jnp.where(off, a_b[c], 0.0)), t_sp[c], NN, ps["inv"])) for c in rng]
        t = [t[c] - _mm(t_sp[c], lt[c], NN, ps["inv"]) for c in rng]
        w *= 2
    t_sp = [_sp(x) for x in t]

    akv = [_mm(a_k[c], v_sp[c], NN, ps["apply"]) for c in rng]
    x_sp = [_sp(_mm(t_sp[c], _sp(jnp.concatenate([kd_bd[c], akv[c]], axis=1)), NN, ps["apply"])) for c in rng]
    bx = [_mm(_sp(_bd(sl(bi_e, c))), x_sp[c], TN, ps["out"]) for c in rng]
    kv = [_mm(_sp(_bd(sl(ki_e, c))), v_sp[c], TN, ps["out"]) for c in rng]
    ry = [_mm(b_b[c], x_sp[c], NN, ps["out"]) for c in rng]
    bkv = [_mm(b_k[c], v_sp[c], NN, ps["out"]) for c in rng]
    m_c = [jnp.where(eye, ect[c * n:c * n + 1, :], 0.0) - bx[c][:, :LANE] for c in rng]
    n_c = [kv[c] - bx[c][:, LANE:] for c in rng]
    r_p = [rd_bd[c] - ry[c][:, :LANE] for c in rng]
    y_p = [bkv[c] - ry[c][:, LANE:] for c in rng]
    return m_c, n_c, r_p, y_p, ones_bd


def _wkv_epilogue(y, r, k, v, g, lnw, lnb, rk, ones_bd):
    inv_n = 1.0 / RW_HD
    mean = _head_sum(y, ones_bd) * inv_n
    yc = y - mean
    var = _head_sum(yc * yc, ones_bd) * inv_n
    yn = yc * lax.rsqrt(var + LNX_EPS) * lnw + lnb
    bonus = _head_sum(r * k * rk, ones_bd) * v
    return (yn + bonus) * g


def _wkv_kernel(r_ref, lw_ref, k_ref, v_ref, kk_ref, a_ref, g_ref, p0_ref, lnw_ref, lnb_ref, rk_ref,
                y_ref, pt_ref, *scratch, nb, npair, carried, ps):
    if carried:
        p_scr, = scratch
        c_idx = pl.program_id(2)

        @pl.when(c_idx == 0)
        def _():
            for pp in range(npair):
                p_scr[pp] = _bd(p0_ref[0, pp])

    terms = []
    for pp in range(npair):
        ln = slice(pp * LANE, (pp + 1) * LANE)
        terms.append(_wkv_terms(r_ref[:, ln], lw_ref[:, ln], k_ref[:, ln], v_ref[:, ln], kk_ref[:, ln],
                                a_ref[:, ln], nb, ps))
    p = [p_scr[pp] for pp in range(npair)] if carried else None
    ys = [[] for _ in range(npair)]
    for c in range(nb):
        for pp in range(npair):
            m_c, n_c, r_p, y_p, _ = terms[pp]
            p_sp = _sp(p[pp] if carried else _bd(p0_ref[c, pp]))
            y_bd = _mm(_sp(r_p[c]), p_sp, NN, ps["seq"]) + y_p[c]
            p_new = _mm(_sp(m_c[c]), p_sp, NN, ps["seq"]) + n_c[c]
            ys[pp].append(y_bd[:RW_HD] + y_bd[RW_HD:])
            if carried:
                p[pp] = p_new
            else:
                pt_ref[c, pp] = p_new[:RW_HD] + p_new[RW_HD:]
    for pp in range(npair):
        ln = slice(pp * LANE, (pp + 1) * LANE)
        y = jnp.concatenate(ys[pp], axis=0) if nb > 1 else ys[pp][0]
        y_ref[:, ln] = _wkv_epilogue(y, r_ref[:, ln], k_ref[:, ln], v_ref[:, ln], g_ref[:, ln], lnw_ref[:, ln],
                                     lnb_ref[:, ln], rk_ref[:, ln], terms[pp][4]).astype(y_ref.dtype)
    if carried:
        for pp in range(npair):
            p_scr[pp] = p[pp]

        @pl.when(c_idx == pl.num_programs(2) - 1)
        def _():
            for pp in range(npair):
                pt_ref[0, pp] = p[pp][:RW_HD] + p[pp][RW_HD:]


def wkv_seq(r, lw, k, v, kk, a, g, pt0, lnw, lnb, rk, *, row0, n_seq, t_seq, nb_pref=16, npair_pref=2):
    width = r.shape[1]
    n_pairs = width // LANE
    npair = _tile(n_pairs, npair_pref, 1)
    blk = _tile(t_seq, nb_pref * CHUNK, CHUNK)
    n_blk = t_seq // blk
    base = row0 // blk
    tok = pl.BlockSpec((blk, npair * LANE), lambda s, p, c: (base + s * n_blk + c, p))
    vec = pl.BlockSpec((1, npair * LANE), lambda s, p, c: (0, p))
    st = pl.BlockSpec((1, npair, RW_HD, LANE), lambda s, p, c: (s, p, 0, 0))
    return pl.pallas_call(
        functools.partial(_wkv_kernel, nb=blk // CHUNK, npair=npair, carried=True, ps=WKV_PASSES),
        grid=(n_seq, n_pairs // npair, n_blk),
        in_specs=[tok] * 7 + [st, vec, vec, vec],
        out_specs=[pl.BlockSpec((blk, npair * LANE), lambda s, p, c: (s * n_blk + c, p)), st],
        out_shape=[jax.ShapeDtypeStruct((n_seq * t_seq, width), BF16),
                   jax.ShapeDtypeStruct(pt0.shape, F32)],
        scratch_shapes=[pltpu.VMEM((npair, LANE, LANE), F32)],
        compiler_params=_params("parallel", "parallel", "arbitrary"),
        name="wkv_seq",
    )(r, lw, k, v, kk, a, g, pt0, lnw, lnb, rk)


def wkv_par(r, lw, k, v, kk, a, g, pt0, lnw, lnb, rk, *, row0, n_seq, nb_pref=16, npair_pref=2):
    width = r.shape[1]
    n_pairs = width // LANE
    npair = _tile(n_pairs, npair_pref, 1)
    nb = _tile(n_seq, nb_pref, 1)
    blk = nb * CHUNK
    base = row0 // blk
    tok = pl.BlockSpec((blk, npair * LANE), lambda s, p: (base + s, p))
    vec = pl.BlockSpec((1, npair * LANE), lambda s, p: (0, p))
    st = pl.BlockSpec((nb, npair, RW_HD, LANE), lambda s, p: (s, p, 0, 0))
    return pl.pallas_call(
        functools.partial(_wkv_kernel, nb=nb, npair=npair, carried=False, ps=WKV_PASSES),
        grid=(n_seq // nb, n_pairs // npair),
        in_specs=[tok] * 7 + [st, vec, vec, vec],
        out_specs=[pl.BlockSpec((blk, npair * LANE), lambda s, p: (s, p)), st],
        out_shape=[jax.ShapeDtypeStruct((n_seq * CHUNK, width), BF16),
                   jax.ShapeDtypeStruct(pt0.shape, F32)],
        compiler_params=_params("parallel", "parallel"),
        name="wkv_par",
    )(r, lw, k, v, kk, a, g, pt0, lnw, lnb, rk)


def _rope(x, cos2, sin2):
    half = QK_ROPE // 2
    rot = jnp.concatenate([-x[:, half:], x[:, :half]], axis=1)
    return x * cos2 + rot * sin2


def _q_prep_kernel(pq_ref, g_ref, w_ref, cos_ref, sin_ref, gn_ref, gr_ref, o_ref, *, n_heads):
    x = pq_ref[...]
    ms = jnp.mean(x * x, axis=-1, keepdims=True)
    xn = (x * lax.rsqrt(ms + NORM_EPS) * g_ref[...]).astype(BF16)
    cos2 = cos_ref[...]
    sin2 = sin_ref[...]
    gn = gn_ref[...]
    gr = gr_ref[...]
    for h in range(n_heads):
        q = jnp.dot(xn, w_ref[h], preferred_element_type=F32)
        nope = q[:, :QK_NOPE]
        rp = _rope(q[:, QK_NOPE:], cos2, sin2)
        ss = jnp.sum(nope * nope, axis=-1, keepdims=True) + jnp.sum(rp * rp, axis=-1, keepdims=True)
        scale = lax.rsqrt(ss * (1.0 / QK_DIM) + NORM_EPS) * QK_SCALE
        o_ref[h] = jnp.concatenate([nope * gn * scale, rp * gr * scale], axis=1).astype(o_ref.dtype)


def q_prep(p_q, g_qa, w_uq_h, cos2, sin2, gn, gr):
    m, ql = p_q.shape
    n_heads = w_uq_h.shape[0]
    tm = _tile(m, 256, 16)
    return pl.pallas_call(
        functools.partial(_q_prep_kernel, n_heads=n_heads),
        grid=(m // tm,),
        in_specs=[pl.BlockSpec((tm, ql), lambda i: (i, 0)),
                  pl.BlockSpec((1, ql), lambda i: (0, 0)),
                  pl.BlockSpec((n_heads, ql, QK_DIM), lambda i: (0, 0, 0)),
                  pl.BlockSpec((tm, QK_ROPE), lambda i: (i, 0)),
                  pl.BlockSpec((tm, QK_ROPE), lambda i: (i, 0)),
                  pl.BlockSpec((1, QK_NOPE), lambda i: (0, 0)),
                  pl.BlockSpec((1, QK_ROPE), lambda i: (0, 0))],
        out_specs=pl.BlockSpec((n_heads, tm, QK_DIM), lambda i: (0, i, 0)),
        out_shape=jax.ShapeDtypeStruct((n_heads, m, QK_DIM), BF16),
        compiler_params=_params("parallel"),
        name="q_prep",
    )(p_q, g_qa, w_uq_h, cos2, sin2, gn, gr)


def _kv_prep_kernel(p_ref, g_ref, cos_ref, sin_ref, ckv_ref, kr_ref, *, kv_lora):
    x = p_ref[:, :kv_lora]
    ms = jnp.mean(x * x, axis=-1, keepdims=True)
    ckv_ref[...] = x * lax.rsqrt(ms + NORM_EPS) * g_ref[...]
    kr_ref[...] = _rope(p_ref[:, kv_lora:kv_lora + QK_ROPE], cos_ref[...], sin_ref[...])


def kv_prep(p_kvr, g_kva, cos2, sin2, kv_lora):
    m, wd = p_kvr.shape
    tm = _tile(m, 512, 8)
    return pl.pallas_call(
        functools.partial(_kv_prep_kernel, kv_lora=kv_lora),
        grid=(m // tm,),
        in_specs=[pl.BlockSpec((tm, wd), lambda i: (i, 0)),
                  pl.BlockSpec((1, kv_lora), lambda i: (0, 0)),
                  pl.BlockSpec((tm, QK_ROPE), lambda i: (i, 0)),
                  pl.BlockSpec((tm, QK_ROPE), lambda i: (i, 0))],
        out_specs=[pl.BlockSpec((tm, kv_lora), lambda i: (i, 0)),
                   pl.BlockSpec((tm, QK_ROPE), lambda i: (i, 0))],
        out_shape=[jax.ShapeDtypeStruct((m, kv_lora), F32),
                   jax.ShapeDtypeStruct((m, QK_ROPE), F32)],
        compiler_params=_params("parallel"),
        name="kv_prep",
    )(p_kvr, g_kva, cos2, sin2)


def _kv_expand_kernel(c_ref, kr_ref, w_ref, gn_ref, gr_ref, k_ref, v_ref, *, hb):
    hw = QK_NOPE + V_HD
    kv = jnp.dot(c_ref[...].astype(BF16), w_ref[...], preferred_element_type=F32)
    kr = kr_ref[...]
    ss_r = jnp.sum(kr * kr, axis=-1, keepdims=True)
    krg = kr * gr_ref[...]
    for h in range(hb):
        kn = kv[:, h * hw:h * hw + QK_NOPE]
        rstd = lax.rsqrt((jnp.sum(kn * kn, axis=-1, keepdims=True) + ss_r) * (1.0 / QK_DIM) + NORM_EPS)
        k_ref[h] = jnp.concatenate([kn * gn_ref[...] * rstd, krg * rstd], axis=1).astype(k_ref.dtype)
        v_ref[h] = kv[:, h * hw + QK_NOPE:(h + 1) * hw].astype(v_ref.dtype)


def kv_expand(ckv, krope, w_ukv, gn, gr, n_heads, rows=None):
    kv_lora = ckv.shape[1]
    rws = ckv.shape[0] if rows is None else rows
    tr = _tile(rws, 1024, 16)
    hb = _tile(n_heads, 4, 1)
    hw = QK_NOPE + V_HD
    return pl.pallas_call(
        functools.partial(_kv_expand_kernel, hb=hb),
        grid=(rws // tr, n_heads // hb),
        in_specs=[pl.BlockSpec((tr, kv_lora), lambda i, h: (i, 0)),
                  pl.BlockSpec((tr, QK_ROPE), lambda i, h: (i, 0)),
                  pl.BlockSpec((kv_lora, hb * hw), lambda i, h: (0, h)),
                  pl.BlockSpec((1, QK_NOPE), lambda i, h: (0, 0)),
                  pl.BlockSpec((1, QK_ROPE), lambda i, h: (0, 0))],
        out_specs=[pl.BlockSpec((hb, tr, QK_DIM), lambda i, h: (h, i, 0)),
                   pl.BlockSpec((hb, tr, V_HD), lambda i, h: (h, i, 0))],
        out_shape=[jax.ShapeDtypeStruct((n_heads, rws, QK_DIM), BF16),
                   jax.ShapeDtypeStruct((n_heads, rws, V_HD), BF16)],
        compiler_params=_params("parallel", "arbitrary"),
        name="kv_expand",
    )(ckv, krope, w_ukv, gn, gr)


def _attn_prompt_kernel(q_ref, k_ref, v_ref, o_ref, *, blk):
    i = pl.program_id(2)
    q = q_ref[0]

    def step(kb, vb, carry, mask):
        m, l, acc = carry
        s = lax.dot_general(q, kb, NT, preferred_element_type=F32)
        if mask is not None:
            s = jnp.where(mask, s, -1e30)
        m_new = jnp.maximum(m, jnp.max(s, axis=-1, keepdims=True))
        p = jnp.exp(s - m_new)
        alpha = jnp.exp(m - m_new)
        l = alpha * l + jnp.sum(p, axis=-1, keepdims=True)
        acc = alpha * acc + jnp.dot(p.astype(BF16), vb, preferred_element_type=F32)
        return m_new, l, acc

    def body(j, carry):
        off = pl.multiple_of(j * blk, blk)
        return step(k_ref[0, pl.ds(off, blk), :], v_ref[0, pl.ds(off, blk), :], carry, None)

    init = (jnp.full((blk, 1), -1e30, F32), jnp.zeros((blk, 1), F32), jnp.zeros((blk, V_HD), F32))
    carry = lax.fori_loop(0, i, body, init)
    qc = lax.broadcasted_iota(jnp.int32, (blk, blk), 0) // CHUNK
    kc = lax.broadcasted_iota(jnp.int32, (blk, blk), 1) // CHUNK
    off = pl.multiple_of(i * blk, blk)
    m, l, acc = step(k_ref[0, pl.ds(off, blk), :], v_ref[0, pl.ds(off, blk), :], carry, kc <= qc)
    o_ref[...] = (acc / l).astype(o_ref.dtype)


def attn_prompt(q, k, v, *, n_seq, t_seq):
    n_heads = q.shape[0]
    blk = _tile(t_seq, 1024, CHUNK)
    nq = t_seq // blk
    return pl.pallas_call(
        functools.partial(_attn_prompt_kernel, blk=blk),
        grid=(n_seq, n_heads, nq),
        in_specs=[pl.BlockSpec((1, blk, QK_DIM), lambda b, h, i: (h, b * nq + i, 0)),
                  pl.BlockSpec((1, t_seq, QK_DIM), lambda b, h, i: (h, b, 0)),
                  pl.BlockSpec((1, t_seq, V_HD), lambda b, h, i: (h, b, 0))],
        out_specs=pl.BlockSpec((blk, V_HD), lambda b, h, i: (b * nq + i, h)),
        out_shape=jax.ShapeDtypeStruct((n_seq * t_seq, n_heads * V_HD), BF16),
        compiler_params=_params("parallel", "parallel", "arbitrary"),
        name="attn_prompt",
    )(q, k, v)


def _attn_sample_kernel(q_ref, k_ref, v_ref, o_ref, *, hb):
    for h in range(hb):
        s = lax.dot_general(q_ref[h], k_ref[h], NT, preferred_element_type=F32)
        m = jnp.max(s, axis=-1, keepdims=True)
        p = jnp.exp(s - m)
        l = jnp.sum(p, axis=-1, keepdims=True)
        acc = jnp.dot(p.astype(BF16), v_ref[h], preferred_element_type=F32)
        o_ref[:, h * V_HD:(h + 1) * V_HD] = (acc / l).astype(o_ref.dtype)


def attn_sample(q, k, v, *, row0, n_seq, t_q, t_kv):
    n_heads = q.shape[0]
    hb = _tile(n_heads, 4, 1)
    base = row0 // t_q
    return pl.pallas_call(
        functools.partial(_attn_sample_kernel, hb=hb),
        grid=(n_seq, n_heads // hb),
        in_specs=[pl.BlockSpec((hb, t_q, QK_DIM), lambda b, h: (h, base + b, 0)),
                  pl.BlockSpec((hb, t_kv, QK_DIM), lambda b, h: (h, b, 0)),
                  pl.BlockSpec((hb, t_kv, V_HD), lambda b, h: (h, b, 0))],
        out_specs=pl.BlockSpec((t_q, hb * V_HD), lambda b, h: (b, h)),
        out_shape=jax.ShapeDtypeStruct((n_seq * t_q, n_heads * V_HD), BF16),
        compiler_params=_params("parallel", "parallel"),
        name="attn_sample",
    )(q, k, v)


def _rope_tables(positions):
    half = QK_ROPE // 2
    inv_freq = ROPE_THETA ** (-jnp.arange(half, dtype=F32) / half)
    ang = positions.astype(F32)[:, None] * inv_freq[None, :]
    cos, sin = jnp.cos(ang), jnp.sin(ang)
    return jnp.concatenate([cos, cos], axis=1), jnp.concatenate([sin, sin], axis=1)


def _pad_cols(w, n):
    return jnp.pad(w, [(0, 0)] * (w.ndim - 1) + [(0, n - w.shape[-1])])


def kernel(x_prompt, x_sample, cache_ckv, cache_krope, state_wkv, state_shift, norm_attn_g, w_in, rw_mu, rw_w0, rw_w2, rw_a0, rw_a2, rw_g2, rw_k_k, rw_k_a, rw_r_k, rw_lnx_w, rw_lnx_b, mla_q_a_g, mla_w_uq, mla_kv_a_g, mla_w_ukv, mla_q_norm_g, mla_k_norm_g, w_out, norm_ffn_g, w_gate, w_up, w_down):
    bp, tp, d = x_prompt.shape
    bs, ts, _ = x_sample.shape
    depth = w_in.shape[0]
    past = cache_ckv.shape[2]
    kv_lora = cache_ckv.shape[3]
    rw_h = rw_r_k.shape[1]
    rw_w = rw_h * RW_HD
    w_lora, a_lora, g_lora = rw_w2.shape[1], rw_a2.shape[1], rw_g2.shape[1]
    lora = w_lora + a_lora + g_lora
    lora_w = -(-lora // LANE) * LANE
    rw_cols = 3 * rw_w + lora
    rw_pad = 3 * rw_w + lora_w
    q_lora = mla_w_uq.shape[1]
    mla_h = mla_w_uq.shape[2] // QK_DIM
    assert (3 * rw_w) % lora_w == 0 and tp % CHUNK == 0 and ts == CHUNK
    mp, ms = bp * tp, bs * ts
    m = mp + ms
    kvr_w = -(-(kv_lora + QK_ROPE) // LANE) * LANE

    layers = range(depth)
    w_rw = [_pad_cols(w_in[l, :, :rw_cols], rw_pad).astype(BF16) for l in layers]
    w_q = [w_in[l, :, rw_cols:rw_cols + q_lora].astype(BF16) for l in layers]
    w_kvr = [_pad_cols(w_in[l, :, rw_cols + q_lora:], kvr_w).astype(BF16) for l in layers]
    mu_p = _pad_cols(rw_mu, rw_pad).reshape(depth, 1, rw_pad)
    zl = lambda w, r0: jnp.zeros((depth, lora_w, rw_w), F32).at[:, r0:r0 + w.shape[1]].set(w).astype(BF16)
    w2p, a2p, g2p = zl(rw_w2, 0), zl(rw_a2, w_lora), zl(rw_g2, w_lora + a_lora)
    vec = lambda t: t.reshape(depth, 1, -1)
    w_uq_h = [mla_w_uq[l].reshape(q_lora, mla_h, QK_DIM).transpose(1, 0, 2).astype(BF16) for l in layers]
    w_ukv = [mla_w_ukv[l].astype(BF16) for l in layers]
    w_o1 = [w_out[l, :rw_w].astype(BF16) for l in layers]
    w_o2 = [w_out[l, rw_w:].astype(BF16) for l in layers]
    w_g = [w_gate[l].astype(BF16) for l in layers]
    w_u = [w_up[l].astype(BF16) for l in layers]
    w_d = [w_down[l].astype(BF16) for l in layers]
    qn_n, qn_r = mla_q_norm_g[:, None, :QK_NOPE], jnp.tile(mla_q_norm_g[:, None, QK_NOPE:], (1, 1, 2))
    kn_n, kn_r = mla_k_norm_g[:, None, :QK_NOPE], jnp.tile(mla_k_norm_g[:, None, QK_NOPE:], (1, 1, 2))

    pos = jnp.concatenate([jnp.tile(jnp.arange(tp), bp), jnp.tile(past + jnp.arange(ts), bs)])
    cos2, sin2 = _rope_tables(pos)

    x = jnp.concatenate([x_prompt.reshape(mp, d), x_sample.reshape(ms, d)], axis=0)
    zero_shift = jnp.zeros((bp, rw_pad), F32)
    to_pairs = lambda s: (s.reshape(-1, rw_h // 2, 2, RW_HD, RW_HD).transpose(0, 1, 4, 2, 3)
                          .reshape(-1, rw_h // 2, RW_HD, 2 * RW_HD))
    from_pairs = lambda p: (p.reshape(-1, rw_h // 2, RW_HD, 2, RW_HD).transpose(0, 1, 3, 4, 2)
                            .reshape(-1, rw_h, RW_HD, RW_HD))
    zero_state = jnp.zeros((bp, rw_h // 2, RW_HD, 2 * RW_HD), F32)
    nbp = tp // CHUNK
    outs = [[] for _ in range(8)]

    for l in range(depth):
        h = rmsnorm_bf16(x, norm_attn_g[l])
        p_rw, last = matmul_with_block_ends(h, w_rw[l])
        p_q = matmul(h, w_q[l], tn_pref=q_lora)
        p_kvr = matmul(h, w_kvr[l], tn_pref=kvr_w)

        lp = last[:mp // CHUNK].reshape(bp, nbp, rw_pad)
        bnd_p = jnp.concatenate([zero_shift[:, None], lp[:, :-1]], axis=1).reshape(bp * nbp, rw_pad)
        bnd = jnp.concatenate([bnd_p, _pad_cols(state_shift[l], rw_pad)], axis=0)
        r, lw, k, v, kk, a, g = rw_prep(p_rw, bnd, mu_p[l], vec(rw_w0)[l], vec(rw_a0)[l], vec(rw_k_k)[l],
                                         vec(rw_k_a)[l], w2p[l], a2p[l], g2p[l], rw_w, lora_w)
        lnw, lnb, rk = vec(rw_lnx_w)[l], vec(rw_lnx_b)[l], rw_r_k[l].reshape(1, rw_w)
        y_rw_p, pst_p = wkv_seq(r, lw, k, v, kk, a, g, zero_state, lnw, lnb, rk, row0=0, n_seq=bp, t_seq=tp)
        y_rw_s, pst_s = wkv_par(r, lw, k, v, kk, a, g, to_pairs(state_wkv[l]), lnw, lnb, rk, row0=mp, n_seq=bs)

        q = q_prep(p_q, vec(mla_q_a_g)[l], w_uq_h[l], cos2, sin2, qn_n[l], qn_r[l])
        ckv_new, kr_new = kv_prep(p_kvr, vec(mla_kv_a_g)[l], cos2, sin2, kv_lora)
        k_p, v_p = kv_expand(ckv_new, kr_new, w_ukv[l], kn_n[l], kn_r[l], mla_h, rows=mp)
        y_mla_p = attn_prompt(q, k_p, v_p, n_seq=bp, t_seq=tp)
        ckv_s = jnp.concatenate([cache_ckv[l], ckv_new[mp:].reshape(bs, ts, kv_lora)], axis=1)
        kr_s = jnp.concatenate([cache_krope[l], kr_new[mp:].reshape(bs, ts, QK_ROPE)], axis=1)
        t_kv = past + ts
        k_s, v_s = kv_expand(ckv_s.reshape(bs * t_kv, kv_lora), kr_s.reshape(bs * t_kv, QK_ROPE),
                             w_ukv[l], kn_n[l], kn_r[l], mla_h)
        y_mla_s = attn_sample(q, k_s, v_s, row0=mp, n_seq=bs, t_q=ts, t_kv=t_kv)

        x = out_proj(y_rw_p, y_rw_s, y_mla_p, y_mla_s, w_o1[l], w_o2[l], x)

        h2 = rmsnorm_bf16(x, norm_ffn_g[l])
        u = ffn_up(h2, w_g[l], w_u[l])
        x = matmul_res_ksplit(u, w_d[l], x)

        outs[0].append(ckv_new[:mp].reshape(bp, tp, kv_lora))
        outs[1].append(kr_new[:mp].reshape(bp, tp, QK_ROPE))
        outs[2].append(from_pairs(pst_p))
        outs[3].append(p_rw[tp - 1:mp:tp, :rw_cols])
        outs[4].append(ckv_new[mp:].reshape(bs, ts, kv_lora))
        outs[5].append(kr_new[mp:].reshape(bs, ts, QK_ROPE))
        outs[6].append(from_pairs(pst_s))
        outs[7].append(p_rw[mp + ts - 1::ts, :rw_cols])

    return (x[:mp].reshape(bp, tp, d), x[mp:].reshape(bs, ts, d)) + tuple(jnp.stack(o) for o in outs)
```

```python
import functools
import math

import jax
import jax.numpy as jnp
from jax import lax
from jax.experimental import pallas as pl
from jax.experimental.pallas import tpu as pltpu

F32 = jnp.float32
BF16 = jnp.bfloat16

CHUNK = 64
RW_HD = 64
QK_NOPE = 128
QK_ROPE = 64
QK_DIM = QK_NOPE + QK_ROPE
V_HD = 128
ROPE_THETA = 10000.0
NORM_EPS = 1e-6
LNX_EPS = 64e-5
QK_SCALE = QK_DIM ** -0.5

LANE = 128
VMEM_LIMIT = 56 * 1024 * 1024

NN = (((1,), (0,)), ((), ()))
NT = (((1,), (1,)), ((), ()))
TN = (((0,), (0,)), ((), ()))


def _tile(n, pref, quantum):
    if n <= pref:
        return n
    t = (pref // quantum) * quantum
    while t >= quantum:
        if n % t == 0:
            return t
        t -= quantum
    raise ValueError(f"no tile for {n} with quantum {quantum}")


def _layer_spec(block, index, layer):
    return pl.BlockSpec((None,) + block, lambda *g: (layer,) + index(*g))


def _params(*sem):
    return pltpu.CompilerParams(dimension_semantics=sem, vmem_limit_bytes=VMEM_LIMIT)


def _rmsnorm_kernel(x_ref, g_ref, o_ref):
    x = x_ref[...]
    ms = jnp.mean(x * x, axis=-1, keepdims=True)
    o_ref[...] = (x * lax.rsqrt(ms + NORM_EPS) * g_ref[...]).astype(o_ref.dtype)


def rmsnorm_bf16(x, g):
    m, d = x.shape
    tm = _tile(m, 256, 8)
    return pl.pallas_call(
        _rmsnorm_kernel,
        grid=(m // tm,),
        in_specs=[pl.BlockSpec((tm, d), lambda i: (i, 0)),
                  pl.BlockSpec((1, d), lambda i: (0, 0))],
        out_specs=pl.BlockSpec((tm, d), lambda i: (i, 0)),
        out_shape=jax.ShapeDtypeStruct((m, d), BF16),
        compiler_params=_params("parallel"),
        name="rmsnorm",
    )(x, g.reshape(1, d))


def _mm_kernel(x_ref, w_ref, o_ref):
    o_ref[...] = jnp.dot(x_ref[...], w_ref[...], preferred_element_type=F32).astype(o_ref.dtype)


def _mm_res_kernel(x_ref, w_ref, r_ref, o_ref):
    o_ref[...] = r_ref[...] + jnp.dot(x_ref[...], w_ref[...], preferred_element_type=F32)


def _out_proj_kernel(y1p_ref, y1s_ref, y2p_ref, y2s_ref, w1_ref, w2_ref, r_ref, o_ref, *, n_p):
    def go(y1_ref, y2_ref):
        acc = jnp.dot(y1_ref[...], w1_ref[...], preferred_element_type=F32)
        acc = acc + jnp.dot(y2_ref[...], w2_ref[...], preferred_element_type=F32)
        o_ref[...] = r_ref[...] + acc

    @pl.when(pl.program_id(0) < n_p)
    def _():
        go(y1p_ref, y2p_ref)

    @pl.when(pl.program_id(0) >= n_p)
    def _():
        go(y1s_ref, y2s_ref)


def _ffn_up_kernel(h_ref, wg_ref, wu_ref, o_ref):
    h = h_ref[...]
    a = jnp.dot(h, wg_ref[...], preferred_element_type=F32)
    b = jnp.dot(h, wu_ref[...], preferred_element_type=F32)
    o_ref[...] = (a * jax.nn.sigmoid(a) * b).astype(o_ref.dtype)


def matmul(x, w, *, tm_pref=1024, tn_pref=512, out_dtype=F32, res=None):
    m, k = x.shape
    n = w.shape[1]
    tm = _tile(m, tm_pref, 8)
    tn = _tile(n, tn_pref, LANE)
    in_specs = [pl.BlockSpec((tm, k), lambda i, j: (i, 0)),
                pl.BlockSpec((k, tn), lambda i, j: (0, j))]
    args = [x, w]
    body = _mm_kernel
    if res is not None:
        in_specs.append(pl.BlockSpec((tm, tn), lambda i, j: (i, j)))
        args.append(res)
        body = _mm_res_kernel
    return pl.pallas_call(
        body,
        grid=(m // tm, n // tn),
        in_specs=in_specs,
        out_specs=pl.BlockSpec((tm, tn), lambda i, j: (i, j)),
        out_shape=jax.ShapeDtypeStruct((m, n), out_dtype),
        compiler_params=_params("parallel", "arbitrary"),
        name="matmul",
    )(*args)


def _mm_res_ksplit_kernel(x_ref, w_ref, r_ref, o_ref):
    part = jnp.dot(x_ref[...], w_ref[...], preferred_element_type=F32)

    @pl.when(pl.program_id(2) == 0)
    def _():
        o_ref[...] = r_ref[...] + part

    @pl.when(pl.program_id(2) > 0)
    def _():
        o_ref[...] += part


def matmul_res_ksplit(x, w, layer, res, *, n_k=2, tm_pref=1024, tn_pref=512):
    m, k = x.shape
    n = w.shape[2]
    tm = _tile(m, tm_pref, 8)
    tn = _tile(n, tn_pref, LANE)
    assert k % (n_k * LANE) == 0
    tk = k // n_k
    return pl.pallas_call(
        _mm_res_ksplit_kernel,
        grid=(m // tm, n // tn, n_k),
        in_specs=[pl.BlockSpec((tm, tk), lambda i, j, kk: (i, kk)),
                  _layer_spec((tk, tn), lambda i, j, kk: (kk, j), layer),
                  pl.BlockSpec((tm, tn), lambda i, j, kk: (i, j))],
        out_specs=pl.BlockSpec((tm, tn), lambda i, j, kk: (i, j)),
        out_shape=jax.ShapeDtypeStruct((m, n), F32),
        compiler_params=_params("parallel", "arbitrary", "arbitrary"),
        name="matmul_ksplit",
    )(x, w, res)


def _mm_last_kernel(x_ref, w_ref, o_ref, b_ref):
    o_ref[...] = jnp.dot(x_ref[...], w_ref[...], preferred_element_type=F32)
    for s in range(b_ref.shape[0]):
        b_ref[s:s + 1, :] = o_ref[CHUNK * s + CHUNK - 1:CHUNK * (s + 1), :]


def matmul_with_block_ends(x, w, layer):
    m, k = x.shape
    n = w.shape[2]
    tm = _tile(m, 1024, 8 * CHUNK)
    tn = _tile(n, 512, LANE)
    return pl.pallas_call(
        _mm_last_kernel,
        grid=(m // tm, n // tn),
        in_specs=[pl.BlockSpec((tm, k), lambda i, j: (i, 0)),
                  _layer_spec((k, tn), lambda i, j: (0, j), layer)],
        out_specs=[pl.BlockSpec((tm, tn), lambda i, j: (i, j)),
                   pl.BlockSpec((tm // CHUNK, tn), lambda i, j: (i, j))],
        out_shape=[jax.ShapeDtypeStruct((m, n), F32), jax.ShapeDtypeStruct((m // CHUNK, n), F32)],
        compiler_params=_params("parallel", "arbitrary"),
        name="matmul_ends",
    )(x, w)


def out_proj(y1p, y1s, y2p, y2s, w, layer, res):
    mp, k1 = y1p.shape
    ms, k2 = y2s.shape
    n = w.shape[2]
    assert k1 == k2 and w.shape[1] == k1 + k2
    tm = _tile(math.gcd(mp, ms), 1024, 8)
    tn = _tile(n, 512, LANE)
    n_p = mp // tm
    row_p = lambda i, j: (jnp.minimum(i, n_p - 1), 0)
    row_s = lambda i, j: (jnp.maximum(i - n_p, 0), 0)
    return pl.pallas_call(
        functools.partial(_out_proj_kernel, n_p=n_p),
        grid=((mp + ms) // tm, n // tn),
        in_specs=[pl.BlockSpec((tm, k1), row_p), pl.BlockSpec((tm, k1), row_s),
                  pl.BlockSpec((tm, k2), row_p), pl.BlockSpec((tm, k2), row_s),
                  _layer_spec((k1, tn), lambda i, j: (0, j), layer),
                  _layer_spec((k2, tn), lambda i, j: (1, j), layer),
                  pl.BlockSpec((tm, tn), lambda i, j: (i, j))],
        out_specs=pl.BlockSpec((tm, tn), lambda i, j: (i, j)),
        out_shape=jax.ShapeDtypeStruct((mp + ms, n), F32),
        compiler_params=_params("parallel", "arbitrary"),
        name="out_proj",
    )(y1p, y1s, y2p, y2s, w, w, res)


def ffn_up(h, wg, wu, layer):
    m, k = h.shape
    n = wg.shape[2]
    tm = _tile(m, 1024, 8)
    tn = _tile(n, 256, LANE)
    return pl.pallas_call(
        _ffn_up_kernel,
        grid=(m // tm, n // tn),
        in_specs=[pl.BlockSpec((tm, k), lambda i, j: (i, 0)),
                  _layer_spec((k, tn), lambda i, j: (0, j), layer),
                  _layer_spec((k, tn), lambda i, j: (0, j), layer)],
        out_specs=pl.BlockSpec((tm, tn), lambda i, j: (i, j)),
        out_shape=jax.ShapeDtypeStruct((m, n), BF16),
        compiler_params=_params("parallel", "arbitrary"),
        name="ffn_up",
    )(h, wg, wu)


INV_BASE = 8

WKV_PASSES = dict(g=1, inv=1, apply=1, out=1, seq=1)


def _sp(x):
    hi = x.astype(BF16)
    return hi, (x - hi.astype(F32)).astype(BF16)


def _mm(a, b, dims, passes):
    d = functools.partial(lax.dot_general, dimension_numbers=dims, preferred_element_type=F32)
    out = d(a[0], b[0])
    if passes == 3:
        out = out + (d(a[0], b[1]) + d(a[1], b[0]))
    return out


def _head_sum(x, ones_bd):
    hi, lo = _sp(x)
    return (jnp.dot(hi, ones_bd, preferred_element_type=F32) + jnp.dot(lo, ones_bd, preferred_element_type=F32))


def _bd(x):
    lane = lax.broadcasted_iota(jnp.int32, x.shape, 1)
    return jnp.concatenate([jnp.where(lane < RW_HD, x, 0.0), jnp.where(lane >= RW_HD, x, 0.0)], axis=0)


def _wkv_terms(r, lw, k, v, kkr, a, nb, ps):
    n = CHUNK
    blk = nb * n
    rin = lax.broadcasted_iota(jnp.int32, (blk, LANE), 0) % n
    r2 = lax.broadcasted_iota(jnp.int32, (LANE, LANE), 0)
    c2 = lax.broadcasted_iota(jnp.int32, (LANE, LANE), 1)
    same = (r2 // RW_HD) == (c2 // RW_HD)
    strict = same & ((c2 % RW_HD) < (r2 % RW_HD))
    incl = same & ((c2 % RW_HD) <= (r2 % RW_HD))
    eye = r2 == c2
    ones_bd = jnp.where(same, 1.0, 0.0).astype(BF16)

    kk = kkr * lax.rsqrt(jnp.maximum(_head_sum(kkr * kkr, ones_bd), 1e-24))
    b = kk * a
    cl = lw
    s = 1
    while s < n:
        cl = cl + jnp.where(rin >= s, pltpu.roll(cl, s, axis=0), 0.0)
        s *= 2
    ct = jnp.broadcast_to(cl.reshape(nb, n, LANE)[:, n - 1:n, :], (nb, n, LANE)).reshape(blk, LANE)
    kd = kk * jnp.exp(cl - lw)
    rd = r * jnp.exp(cl)
    einv = jnp.exp(-cl)
    ki = k * einv
    bi = b * einv
    e_end = jnp.exp(ct - cl)
    ki_e = k * e_end
    bi_e = b * e_end
    ect = jnp.exp(ct)

    rng = range(nb)
    sl = lambda x, c: x[c * n:(c + 1) * n]
    kd_bd = [_bd(sl(kd, c)) for c in rng]
    rd_bd = [_bd(sl(rd, c)) for c in rng]
    v_sp = [_sp(_bd(sl(v, c))) for c in rng]
    kr_sp = [_sp(jnp.concatenate([kd_bd[c], rd_bd[c]], axis=0)) for c in rng]
    bi_sp = [_sp(_bd(sl(bi, c))) for c in rng]
    ki_sp = [_sp(_bd(sl(ki, c))) for c in rng]
    g1 = [_mm(kr_sp[c], bi_sp[c], NT, ps["g"]) for c in rng]
    g2 = [_mm(kr_sp[c], ki_sp[c], NT, ps["g"]) for c in rng]
    a_k = [_sp(jnp.where(strict, g2[c][:LANE], 0.0)) for c in rng]
    b_k = [_sp(jnp.where(incl, g2[c][LANE:], 0.0)) for c in rng]
    b_b = [_sp(jnp.where(incl, g1[c][LANE:], 0.0)) for c in rng]

    a_b = [jnp.where(strict, g1[c][:LANE], 0.0) for c in rng]
    blk_mask = lambda w: (r2 // w) == (c2 // w)
    n0 = [jnp.where(blk_mask(INV_BASE), -a_b[c], 0.0) for c in rng]
    t = [jnp.where(eye, 1.0, 0.0) + n0[c] for c in rng]
    nk = [_sp(x) for x in n0]
    s = 2
    while s < INV_BASE:
        nk = [_sp(_mm(nk[c], nk[c], NN, ps["inv"])) for c in rng]
        t = [t[c] + _mm(_sp(t[c]), nk[c], NN, ps["inv"]) for c in rng]
        s *= 2
    w = INV_BASE
    while w < n:
        off = blk_mask(2 * w) & jnp.logical_not(blk_mask(w))
        t_sp = [_sp(x) for x in t]
        lt = [_sp(_mm(_sp(jnp.where(off, a_b[c], 0.0)), t_sp[c], NN, ps["inv"])) for c in rng]
        t = [t[c] - _mm(t_sp[c], lt[c], NN, ps["inv"]) for c in rng]
        w *= 2
    t_sp = [_sp(x) for x in t]

    akv = [_mm(a_k[c], v_sp[c], NN, ps["apply"]) for c in rng]
    x_sp = [_sp(_mm(t_sp[c], _sp(jnp.concatenate([kd_bd[c], akv[c]], axis=1)), NN, ps["apply"])) for c in rng]
    bx = [_mm(_sp(_bd(sl(bi_e, c))), x_sp[c], TN, ps["out"]) for c in rng]
    kv = [_mm(_sp(_bd(sl(ki_e, c))), v_sp[c], TN, ps["out"]) for c in rng]
    ry = [_mm(b_b[c], x_sp[c], NN, ps["out"]) for c in rng]
    bkv = [_mm(b_k[c], v_sp[c], NN, ps["out"]) for c in rng]
    m_c = [jnp.where(eye, ect[c * n:c * n + 1, :], 0.0) - bx[c][:, :LANE] for c in rng]
    n_c = [kv[c] - bx[c][:, LANE:] for c in rng]
    r_p = [rd_bd[c] - ry[c][:, :LANE] for c in rng]
    y_p = [bkv[c] - ry[c][:, LANE:] for c in rng]
    return m_c, n_c, r_p, y_p, ones_bd


def _wkv_epilogue(y, r, k, v, g, lnw, lnb, rk, ones_bd):
    inv_n = 1.0 / RW_HD
    mean = _head_sum(y, ones_bd) * inv_n
    yc = y - mean
    var = _head_sum(yc * yc, ones_bd) * inv_n
    yn = yc * lax.rsqrt(var + LNX_EPS) * lnw + lnb
    bonus = _head_sum(r * k * rk, ones_bd) * v
    return (yn + bonus) * g


def _rwkv_kernel(pr_ref, pk_ref, pv_ref, pl_ref, br_ref, bk_ref, bv_ref, bl_ref, mur_ref, muk_ref, muv_ref, mul_ref,
                 w0_ref, a0_ref, kk_ref, ka_ref, w2_ref, a2_ref, g2_ref, p0_ref, lnw_ref, lnb_ref, rk_ref,
                 y_ref, pt_ref, *scratch, nb, npair, carried, lora_tiles, ps):
    if carried:
        p_scr, = scratch
        c_idx = pl.program_id(2)

        @pl.when(c_idx == 0)
        def _():
            for pp in range(npair):
                p_scr[pp] = _bd(p0_ref[0, pp])

    blk = nb * CHUNK
    run = blk if carried else CHUNK
    first = lax.broadcasted_iota(jnp.int32, (run, 1), 0) == 0

    def shifted(p_ref, b_ref, mu_ref):
        mu = mu_ref[...]
        outs = []
        for s in range(blk // run):
            sub = p_ref[run * s:run * (s + 1), :]
            prev = jnp.where(first, b_ref[s:s + 1, :], pltpu.roll(sub, 1, axis=0))
            outs.append(sub + (prev - sub) * mu)
        return outs[0] if len(outs) == 1 else jnp.concatenate(outs, axis=0)

    r = shifted(pr_ref, br_ref, mur_ref)
    xk = shifted(pk_ref, bk_ref, muk_ref)
    v = shifted(pv_ref, bv_ref, muv_ref)
    xl = shifted(pl_ref, bl_ref, mul_ref)
    (w_lo, w_hi), (a_lo, a_hi), (g_lo, g_hi) = lora_tiles
    dw = jnp.dot(jnp.tanh(xl[:, w_lo:w_hi]).astype(BF16), w2_ref[w_lo:w_hi, :], preferred_element_type=F32)
    da = jnp.dot(xl[:, a_lo:a_hi].astype(BF16), a2_ref[a_lo:a_hi, :], preferred_element_type=F32)
    g = jnp.dot(jax.nn.sigmoid(xl[:, g_lo:g_hi]).astype(BF16), g2_ref[g_lo:g_hi, :], preferred_element_type=F32)
    w_log = -jax.nn.softplus(-(w0_ref[...] + dw)) - 0.5
    lw = -jnp.exp(w_log)
    a = jax.nn.sigmoid(a0_ref[...] + da)
    k = xk * (1.0 + (a - 1.0) * ka_ref[...])
    kkr = xk * kk_ref[...]

    terms = []
    for pp in range(npair):
        ln = slice(pp * LANE, (pp + 1) * LANE)
        terms.append(_wkv_terms(r[:, ln], lw[:, ln], k[:, ln], v[:, ln], kkr[:, ln], a[:, ln], nb, ps))
    p = [p_scr[pp] for pp in range(npair)] if carried else None
    ys = [[] for _ in range(npair)]
    for c in range(nb):
        for pp in range(npair):
            m_c, n_c, r_p, y_p, _ = terms[pp]
            p_sp = _sp(p[pp] if carried else _bd(p0_ref[c, pp]))
            y_bd = _mm(_sp(r_p[c]), p_sp, NN, ps["seq"]) + y_p[c]
            p_new = _mm(_sp(m_c[c]), p_sp, NN, ps["seq"]) + n_c[c]
            ys[pp].append(y_bd[:RW_HD] + y_bd[RW_HD:])
            if carried:
                p[pp] = p_new
            else:
                pt_ref[c, pp] = p_new[:RW_HD] + p_new[RW_HD:]
    for pp in range(npair):
        ln = slice(pp * LANE, (pp + 1) * LANE)
        y = jnp.concatenate(ys[pp], axis=0) if nb > 1 else ys[pp][0]
        y_ref[:, ln] = _wkv_epilogue(y, r[:, ln], k[:, ln], v[:, ln], g[:, ln], lnw_ref[:, ln],
                                     lnb_ref[:, ln], rk_ref[:, ln], terms[pp][4]).astype(y_ref.dtype)
    if carried:
        for pp in range(npair):
            p_scr[pp] = p[pp]

        @pl.when(c_idx == pl.num_programs(2) - 1)
        def _():
            for pp in range(npair):
                pt_ref[0, pp] = p[pp][:RW_HD] + p[pp][RW_HD:]


def rwkv_mix(p_rw, last_rows, shift0, mu, w0, a0, k_k, k_a, w2p, a2p, g2p, layer, pt0, lnw, lnb, rk, *,
             row0, n_seq, t_seq, rw_w, lora_w, lora_dims, nb_pref=16, npair_pref=2):
    carried = t_seq > CHUNK
    cols = p_rw.shape[1]
    n_pairs = rw_w // LANE
    npair = _tile(n_pairs, npair_pref, 1)
    gw = npair * LANE
    ncb = rw_w // gw
    lora_blk = (3 * rw_w) // lora_w
    if carried:
        blk = _tile(t_seq, nb_pref * CHUNK, CHUNK)
        nb, n_blk = blk // CHUNK, t_seq // blk
        grid = (n_seq, ncb, n_blk)
        step = lambda *g: g[0] * n_blk + g[2]
        ends = last_rows.reshape(n_seq, t_seq // CHUNK, cols)[:, nb - 1::nb]
        bnd = jnp.concatenate([shift0[:, None], ends[:, :-1]], axis=1).reshape(n_seq * n_blk, 1, cols)
        st = pl.BlockSpec((1, npair, RW_HD, LANE), lambda *g: (g[0], g[1], 0, 0))
        sem = ("parallel", "parallel", "arbitrary")
        scratch = [pltpu.VMEM((npair, LANE, LANE), F32)]
    else:
        nb = _tile(n_seq, nb_pref, 1)
        blk = nb * CHUNK
        grid = (n_seq // nb, ncb)
        step = lambda *g: g[0]
        bnd = shift0.reshape(n_seq // nb, nb, cols)
        st = pl.BlockSpec((nb, npair, RW_HD, LANE), lambda *g: (g[0], g[1], 0, 0))
        sem = ("parallel", "parallel")
        scratch = []
    base = row0 // blk
    nbnd = bnd.shape[1]
    tile = lambda lo, hi: (lo // LANE * LANE, -(-hi // LANE) * LANE)
    w_l, a_l, g_l = lora_dims
    lora_tiles = (tile(0, w_l), tile(w_l, w_l + a_l), tile(w_l + a_l, w_l + a_l + g_l))

    def tok(sec):
        return pl.BlockSpec((blk, gw), lambda *g: (base + step(*g), sec * ncb + g[1]))

    def bound(sec):
        return pl.BlockSpec((None, nbnd, gw), lambda *g: (step(*g), 0, sec * ncb + g[1]))

    def mus(sec):
        return pl.BlockSpec((1, gw), lambda *g: (0, sec * ncb + g[1]))

    vec = pl.BlockSpec((1, gw), lambda *g: (0, g[1]))
    wsp = _layer_spec((lora_w, gw), lambda *g: (0, g[1]), layer)
    in_specs = ([tok(0), tok(1), tok(2), pl.BlockSpec((blk, lora_w), lambda *g: (base + step(*g), lora_blk))]
                + [bound(0), bound(1), bound(2),
                   pl.BlockSpec((None, nbnd, lora_w), lambda *g: (step(*g), 0, lora_blk))]
                + [mus(0), mus(1), mus(2), pl.BlockSpec((1, lora_w), lambda *g: (0, lora_blk))]
                + [vec] * 4 + [wsp] * 3 + [st, vec, vec, vec])
    return pl.pallas_call(
        functools.partial(_rwkv_kernel, nb=nb, npair=npair, carried=carried, lora_tiles=lora_tiles, ps=WKV_PASSES),
        grid=grid,
        in_specs=in_specs,
        out_specs=[pl.BlockSpec((blk, gw), lambda *g: (step(*g), g[1])), st],
        out_shape=[jax.ShapeDtypeStruct((n_seq * t_seq, rw_w), BF16), jax.ShapeDtypeStruct(pt0.shape, F32)],
        scratch_shapes=scratch,
        compiler_params=_params(*sem),
        name="rwkv_seq" if carried else "rwkv_par",
    )(p_rw, p_rw, p_rw, p_rw, bnd, bnd, bnd, bnd, mu, mu, mu, mu, w0, a0, k_k, k_a, w2p, a2p, g2p,
      pt0, lnw, lnb, rk)


def _rope(x, cos2, sin2):
    half = QK_ROPE // 2
    rot = jnp.concatenate([-x[:, half:], x[:, :half]], axis=1)
    return x * cos2 + rot * sin2


def _q_prep_kernel(pq_ref, g_ref, w_ref, cos_ref, sin_ref, gn_ref, gr_ref, o_ref, *, n_heads):
    x = pq_ref[...]
    ms = jnp.mean(x * x, axis=-1, keepdims=True)
    xn = (x * lax.rsqrt(ms + NORM_EPS) * g_ref[...]).astype(BF16)
    cos2 = cos_ref[...]
    sin2 = sin_ref[...]
    gn = gn_ref[...]
    gr = gr_ref[...]
    for h in range(n_heads):
        q = jnp.dot(xn, w_ref[h], preferred_element_type=F32)
        nope = q[:, :QK_NOPE]
        rp = _rope(q[:, QK_NOPE:], cos2, sin2)
        ss = jnp.sum(nope * nope, axis=-1, keepdims=True) + jnp.sum(rp * rp, axis=-1, keepdims=True)
        scale = lax.rsqrt(ss * (1.0 / QK_DIM) + NORM_EPS) * QK_SCALE
        o_ref[h] = jnp.concatenate([nope * gn * scale, rp * gr * scale], axis=1).astype(o_ref.dtype)


def q_prep(p_q, g_qa, w_uq_h, cos2, sin2, gn, gr):
    m, ql = p_q.shape
    n_heads = w_uq_h.shape[0]
    tm = _tile(m, 256, 16)
    return pl.pallas_call(
        functools.partial(_q_prep_kernel, n_heads=n_heads),
        grid=(m // tm,),
        in_specs=[pl.BlockSpec((tm, ql), lambda i: (i, 0)),
                  pl.BlockSpec((1, ql), lambda i: (0, 0)),
                  pl.BlockSpec((n_heads, ql, QK_DIM), lambda i: (0, 0, 0)),
                  pl.BlockSpec((tm, QK_ROPE), lambda i: (i, 0)),
                  pl.BlockSpec((tm, QK_ROPE), lambda i: (i, 0)),
                  pl.BlockSpec((1, QK_NOPE), lambda i: (0, 0)),
                  pl.BlockSpec((1, QK_ROPE), lambda i: (0, 0))],
        out_specs=pl.BlockSpec((n_heads, tm, QK_DIM), lambda i: (0, i, 0)),
        out_shape=jax.ShapeDtypeStruct((n_heads, m, QK_DIM), BF16),
        compiler_params=_params("parallel"),
        name="q_prep",
    )(p_q, g_qa, w_uq_h, cos2, sin2, gn, gr)


def _kv_prep_kernel(p_ref, g_ref, cos_ref, sin_ref, ckv_ref, kr_ref, *, kv_lora):
    x = p_ref[:, :kv_lora]
    ms = jnp.mean(x * x, axis=-1, keepdims=True)
    ckv_ref[...] = x * lax.rsqrt(ms + NORM_EPS) * g_ref[...]
    kr_ref[...] = _rope(p_ref[:, kv_lora:kv_lora + QK_ROPE], cos_ref[...], sin_ref[...])


def kv_prep(p_kvr, g_kva, cos2, sin2, kv_lora):
    m, wd = p_kvr.shape
    tm = _tile(m, 512, 8)
    return pl.pallas_call(
        functools.partial(_kv_prep_kernel, kv_lora=kv_lora),
        grid=(m // tm,),
        in_specs=[pl.BlockSpec((tm, wd), lambda i: (i, 0)),
                  pl.BlockSpec((1, kv_lora), lambda i: (0, 0)),
                  pl.BlockSpec((tm, QK_ROPE), lambda i: (i, 0)),
                  pl.BlockSpec((tm, QK_ROPE), lambda i: (i, 0))],
        out_specs=[pl.BlockSpec((tm, kv_lora), lambda i: (i, 0)),
                   pl.BlockSpec((tm, QK_ROPE), lambda i: (i, 0))],
        out_shape=[jax.ShapeDtypeStruct((m, kv_lora), F32),
                   jax.ShapeDtypeStruct((m, QK_ROPE), F32)],
        compiler_params=_params("parallel"),
        name="kv_prep",
    )(p_kvr, g_kva, cos2, sin2)


def _kv_expand_kernel(c_ref, kr_ref, w_ref, gn_ref, gr_ref, k_ref, v_ref, *, hb):
    hw = QK_NOPE + V_HD
    kv = jnp.dot(c_ref[...].astype(BF16), w_ref[...], preferred_element_type=F32)
    kr = kr_ref[...]
    ss_r = jnp.sum(kr * kr, axis=-1, keepdims=True)
    krg = kr * gr_ref[...]
    for h in range(hb):
        kn = kv[:, h * hw:h * hw + QK_NOPE]
        rstd = lax.rsqrt((jnp.sum(kn * kn, axis=-1, keepdims=True) + ss_r) * (1.0 / QK_DIM) + NORM_EPS)
        k_ref[h] = jnp.concatenate([kn * gn_ref[...] * rstd, krg * rstd], axis=1).astype(k_ref.dtype)
        v_ref[h] = kv[:, h * hw + QK_NOPE:(h + 1) * hw].astype(v_ref.dtype)


def kv_expand(ckv, krope, w_ukv, layer, gn, gr, n_heads, rows=None):
    kv_lora = ckv.shape[1]
    rws = ckv.shape[0] if rows is None else rows
    tr = _tile(rws, 1024, 16)
    hb = _tile(n_heads, 4, 1)
    hw = QK_NOPE + V_HD
    return pl.pallas_call(
        functools.partial(_kv_expand_kernel, hb=hb),
        grid=(rws // tr, n_heads // hb),
        in_specs=[pl.BlockSpec((tr, kv_lora), lambda i, h: (i, 0)),
                  pl.BlockSpec((tr, QK_ROPE), lambda i, h: (i, 0)),
                  _layer_spec((kv_lora, hb * hw), lambda i, h: (0, h), layer),
                  pl.BlockSpec((1, QK_NOPE), lambda i, h: (0, 0)),
                  pl.BlockSpec((1, QK_ROPE), lambda i, h: (0, 0))],
        out_specs=[pl.BlockSpec((hb, tr, QK_DIM), lambda i, h: (h, i, 0)),
                   pl.BlockSpec((hb, tr, V_HD), lambda i, h: (h, i, 0))],
        out_shape=[jax.ShapeDtypeStruct((n_heads, rws, QK_DIM), BF16),
                   jax.ShapeDtypeStruct((n_heads, rws, V_HD), BF16)],
        compiler_params=_params("parallel", "arbitrary"),
        name="kv_expand",
    )(ckv, krope, w_ukv, gn, gr)


def _attn_prompt_kernel(q_ref, k_ref, v_ref, o_ref, *, blk):
    i = pl.program_id(2)
    q = q_ref[0]

    def step(kb, vb, carry, mask):
        m, l, acc = carry
        s = lax.dot_general(q, kb, NT, preferred_element_type=F32)
        if mask is not None:
            s = jnp.where(mask, s, -1e30)
        m_new = jnp.maximum(m, jnp.max(s, axis=-1, keepdims=True))
        p = jnp.exp(s - m_new)
        alpha = jnp.exp(m - m_new)
        l = alpha * l + jnp.sum(p, axis=-1, keepdims=True)
        acc = alpha * acc + jnp.dot(p.astype(BF16), vb, preferred_element_type=F32)
        return m_new, l, acc

    def body(j, carry):
        off = pl.multiple_of(j * blk, blk)
        return step(k_ref[0, pl.ds(off, blk), :], v_ref[0, pl.ds(off, blk), :], carry, None)

    init = (jnp.full((blk, 1), -1e30, F32), jnp.zeros((blk, 1), F32), jnp.zeros((blk, V_HD), F32))
    carry = lax.fori_loop(0, i, body, init)
    qc = lax.broadcasted_iota(jnp.int32, (blk, blk), 0) // CHUNK
    kc = lax.broadcasted_iota(jnp.int32, (blk, blk), 1) // CHUNK
    off = pl.multiple_of(i * blk, blk)
    m, l, acc = step(k_ref[0, pl.ds(off, blk), :], v_ref[0, pl.ds(off, blk), :], carry, kc <= qc)
    o_ref[...] = (acc / l).astype(o_ref.dtype)


def attn_prompt(q, k, v, *, n_seq, t_seq):
    n_heads = q.shape[0]
    blk = _tile(t_seq, 1024, CHUNK)
    nq = t_seq // blk
    return pl.pallas_call(
        functools.partial(_attn_prompt_kernel, blk=blk),
        grid=(n_seq, n_heads, nq),
        in_specs=[pl.BlockSpec((1, blk, QK_DIM), lambda b, h, i: (h, b * nq + i, 0)),
                  pl.BlockSpec((1, t_seq, QK_DIM), lambda b, h, i: (h, b, 0)),
                  pl.BlockSpec((1, t_seq, V_HD), lambda b, h, i: (h, b, 0))],
        out_specs=pl.BlockSpec((blk, V_HD), lambda b, h, i: (b * nq + i, h)),
        out_shape=jax.ShapeDtypeStruct((n_seq * t_seq, n_heads * V_HD), BF16),
        compiler_params=_params("parallel", "parallel", "arbitrary"),
        name="attn_prompt",
    )(q, k, v)


def _attn_sample_kernel(q_ref, k_ref, v_ref, o_ref, *, hb):
    for h in range(hb):
        s = lax.dot_general(q_ref[h], k_ref[h], NT, preferred_element_type=F32)
        m = jnp.max(s, axis=-1, keepdims=True)
        p = jnp.exp(s - m)
        l = jnp.sum(p, axis=-1, keepdims=True)
        acc = jnp.dot(p.astype(BF16), v_ref[h], preferred_element_type=F32)
        o_ref[:, h * V_HD:(h + 1) * V_HD] = (acc / l).astype(o_ref.dtype)


def attn_sample(q, k, v, *, row0, n_seq, t_q, t_kv):
    n_heads = q.shape[0]
    hb = _tile(n_heads, 4, 1)
    base = row0 // t_q
    return pl.pallas_call(
        functools.partial(_attn_sample_kernel, hb=hb),
        grid=(n_seq, n_heads // hb),
        in_specs=[pl.BlockSpec((hb, t_q, QK_DIM), lambda b, h: (h, base + b, 0)),
                  pl.BlockSpec((hb, t_kv, QK_DIM), lambda b, h: (h, b, 0)),
                  pl.BlockSpec((hb, t_kv, V_HD), lambda b, h: (h, b, 0))],
        out_specs=pl.BlockSpec((t_q, hb * V_HD), lambda b, h: (b, h)),
        out_shape=jax.ShapeDtypeStruct((n_seq * t_q, n_heads * V_HD), BF16),
        compiler_params=_params("parallel", "parallel"),
        name="attn_sample",
    )(q, k, v)


def _rope_tables(positions):
    half = QK_ROPE // 2
    inv_freq = ROPE_THETA ** (-jnp.arange(half, dtype=F32) / half)
    ang = positions.astype(F32)[:, None] * inv_freq[None, :]
    cos, sin = jnp.cos(ang), jnp.sin(ang)
    return jnp.concatenate([cos, cos], axis=1), jnp.concatenate([sin, sin], axis=1)


def _pad_cols(w, n):
    return jnp.pad(w, [(0, 0)] * (w.ndim - 1) + [(0, n - w.shape[-1])])


def kernel(x_prompt, x_sample, cache_ckv, cache_krope, state_wkv, state_shift, norm_attn_g, w_in, rw_mu, rw_w0, rw_w2, rw_a0, rw_a2, rw_g2, rw_k_k, rw_k_a, rw_r_k, rw_lnx_w, rw_lnx_b, mla_q_a_g, mla_w_uq, mla_kv_a_g, mla_w_ukv, mla_q_norm_g, mla_k_norm_g, w_out, norm_ffn_g, w_gate, w_up, w_down):
    bp, tp, d = x_prompt.shape
    bs, ts, _ = x_sample.shape
    depth = w_in.shape[0]
    past = cache_ckv.shape[2]
    kv_lora = cache_ckv.shape[3]
    rw_h = rw_r_k.shape[1]
    rw_w = rw_h * RW_HD
    w_lora, a_lora, g_lora = rw_w2.shape[1], rw_a2.shape[1], rw_g2.shape[1]
    lora = w_lora + a_lora + g_lora
    lora_w = -(-lora // LANE) * LANE
    rw_cols = 3 * rw_w + lora
    rw_pad = 3 * rw_w + lora_w
    q_lora = mla_w_uq.shape[1]
    mla_h = mla_w_uq.shape[2] // QK_DIM
    assert (3 * rw_w) % lora_w == 0 and tp % CHUNK == 0 and ts == CHUNK
    mp, ms = bp * tp, bs * ts
    m = mp + ms
    kvr_w = -(-(kv_lora + QK_ROPE) // LANE) * LANE

    layers = range(depth)
    w_rw = _pad_cols(w_in[:, :, :rw_cols], rw_pad).astype(BF16)
    w_q = [w_in[l, :, rw_cols:rw_cols + q_lora].astype(BF16) for l in layers]
    w_kvr = [_pad_cols(w_in[l, :, rw_cols + q_lora:], kvr_w).astype(BF16) for l in layers]
    mu_p = _pad_cols(rw_mu, rw_pad).reshape(depth, 1, rw_pad)
    zl = lambda w, r0: jnp.zeros((depth, lora_w, rw_w), F32).at[:, r0:r0 + w.shape[1]].set(w).astype(BF16)
    w2p, a2p, g2p = zl(rw_w2, 0), zl(rw_a2, w_lora), zl(rw_g2, w_lora + a_lora)
    vec = lambda t: t.reshape(depth, 1, -1)
    w_uq_h = [mla_w_uq[l].reshape(q_lora, mla_h, QK_DIM).transpose(1, 0, 2).astype(BF16) for l in layers]
    w_ukv, w_o = mla_w_ukv.astype(BF16), w_out.astype(BF16)
    w_g, w_u, w_d = w_gate.astype(BF16), w_up.astype(BF16), w_down.astype(BF16)
    qn_n, qn_r = mla_q_norm_g[:, None, :QK_NOPE], jnp.tile(mla_q_norm_g[:, None, QK_NOPE:], (1, 1, 2))
    kn_n, kn_r = mla_k_norm_g[:, None, :QK_NOPE], jnp.tile(mla_k_norm_g[:, None, QK_NOPE:], (1, 1, 2))

    pos = jnp.concatenate([jnp.tile(jnp.arange(tp), bp), jnp.tile(past + jnp.arange(ts), bs)])
    cos2, sin2 = _rope_tables(pos)

    x = jnp.concatenate([x_prompt.reshape(mp, d), x_sample.reshape(ms, d)], axis=0)
    zero_shift = jnp.zeros((bp, rw_pad), F32)
    to_pairs = lambda s: (s.reshape(-1, rw_h // 2, 2, RW_HD, RW_HD).transpose(0, 1, 4, 2, 3)
                          .reshape(-1, rw_h // 2, RW_HD, 2 * RW_HD))
    from_pairs = lambda p: (p.reshape(-1, rw_h // 2, RW_HD, 2, RW_HD).transpose(0, 1, 3, 4, 2)
                            .reshape(-1, rw_h, RW_HD, RW_HD))
    zero_state = jnp.zeros((bp, rw_h // 2, RW_HD, 2 * RW_HD), F32)
    outs = [[] for _ in range(8)]

    for l in range(depth):
        h = rmsnorm_bf16(x, norm_attn_g[l])
        p_rw, last = matmul_with_block_ends(h, w_rw, l)
        p_q = matmul(h, w_q[l], tn_pref=q_lora)
        p_kvr = matmul(h, w_kvr[l], tn_pref=kvr_w)

        lnw, lnb, rk = vec(rw_lnx_w)[l], vec(rw_lnx_b)[l], rw_r_k[l].reshape(1, rw_w)
        rw_args = (mu_p[l], vec(rw_w0)[l], vec(rw_a0)[l], vec(rw_k_k)[l], vec(rw_k_a)[l], w2p, a2p, g2p, l)
        rw_dims = dict(rw_w=rw_w, lora_w=lora_w, lora_dims=(w_lora, a_lora, g_lora))
        y_rw_p, pst_p = rwkv_mix(p_rw, last[:mp // CHUNK], zero_shift, *rw_args, zero_state, lnw, lnb, rk,
                                 row0=0, n_seq=bp, t_seq=tp, **rw_dims)
        y_rw_s, pst_s = rwkv_mix(p_rw, last[mp // CHUNK:], _pad_cols(state_shift[l], rw_pad), *rw_args,
                                 to_pairs(state_wkv[l]), lnw, lnb, rk, row0=mp, n_seq=bs, t_seq=ts, **rw_dims)

        q = q_prep(p_q, vec(mla_q_a_g)[l], w_uq_h[l], cos2, sin2, qn_n[l], qn_r[l])
        ckv_new, kr_new = kv_prep(p_kvr, vec(mla_kv_a_g)[l], cos2, sin2, kv_lora)
        k_p, v_p = kv_expand(ckv_new, kr_new, w_ukv, l, kn_n[l], kn_r[l], mla_h, rows=mp)
        y_mla_p = attn_prompt(q, k_p, v_p, n_seq=bp, t_seq=tp)
        ckv_s = jnp.concatenate([cache_ckv[l], ckv_new[mp:].reshape(bs, ts, kv_lora)], axis=1)
        kr_s = jnp.concatenate([cache_krope[l], kr_new[mp:].reshape(bs, ts, QK_ROPE)], axis=1)
        t_kv = past + ts
        k_s, v_s = kv_expand(ckv_s.reshape(bs * t_kv, kv_lora), kr_s.reshape(bs * t_kv, QK_ROPE),
                             w_ukv, l, kn_n[l], kn_r[l], mla_h)
        y_mla_s = attn_sample(q, k_s, v_s, row0=mp, n_seq=bs, t_q=ts, t_kv=t_kv)

        x = out_proj(y_rw_p, y_rw_s, y_mla_p, y_mla_s, w_o, l, x)

        h2 = rmsnorm_bf16(x, norm_ffn_g[l])
        u = ffn_up(h2, w_g, w_u, l)
        x = matmul_res_ksplit(u, w_d, l, x)

        outs[0].append(ckv_new[:mp].reshape(bp, tp, kv_lora))
        outs[1].append(kr_new[:mp].reshape(bp, tp, QK_ROPE))
        outs[2].append(from_pairs(pst_p))
        outs[3].append(p_rw[tp - 1:mp:tp, :rw_cols])
        outs[4].append(ckv_new[mp:].reshape(bs, ts, kv_lora))
        outs[5].append(kr_new[mp:].reshape(bs, ts, QK_ROPE))
        outs[6].append(from_pairs(pst_s))
        outs[7].append(p_rw[mp + ts - 1::ts, :rw_cols])

    return (x[:mp].reshape(bp, tp, d), x[mp:].reshape(bs, ts, d)) + tuple(jnp.stack(o) for o in outs)
```

```python
import functools
import math

import jax
import jax.numpy as jnp
from jax import lax
from jax.experimental import pallas as pl
from jax.experimental.pallas import tpu as pltpu

F32 = jnp.float32
BF16 = jnp.bfloat16

CHUNK = 64
RW_HD = 64
QK_NOPE = 128
QK_ROPE = 64
QK_DIM = QK_NOPE + QK_ROPE
V_HD = 128
ROPE_THETA = 10000.0
NORM_EPS = 1e-6
LNX_EPS = 64e-5
QK_SCALE = QK_DIM ** -0.5

LANE = 128
VMEM_LIMIT = 56 * 1024 * 1024

NN = (((1,), (0,)), ((), ()))
NT = (((1,), (1,)), ((), ()))
TN = (((0,), (0,)), ((), ()))


def _tile(n, pref, quantum):
    if n <= pref:
        return n
    t = (pref // quantum) * quantum
    while t >= quantum:
        if n % t == 0:
            return t
        t -= quantum
    raise ValueError(f"no tile for {n} with quantum {quantum}")


def _layer_spec(block, index, layer):
    return pl.BlockSpec((None,) + block, lambda *g: (layer,) + index(*g))


def _params(*sem):
    return pltpu.CompilerParams(dimension_semantics=sem, vmem_limit_bytes=VMEM_LIMIT)


def _rmsnorm_kernel(x_ref, g_ref, o_ref):
    x = x_ref[...]
    ms = jnp.mean(x * x, axis=-1, keepdims=True)
    o_ref[...] = (x * lax.rsqrt(ms + NORM_EPS) * g_ref[...]).astype(o_ref.dtype)


def rmsnorm_bf16(x, g):
    m, d = x.shape
    tm = _tile(m, 256, 8)
    return pl.pallas_call(
        _rmsnorm_kernel,
        grid=(m // tm,),
        in_specs=[pl.BlockSpec((tm, d), lambda i: (i, 0)),
                  pl.BlockSpec((1, d), lambda i: (0, 0))],
        out_specs=pl.BlockSpec((tm, d), lambda i: (i, 0)),
        out_shape=jax.ShapeDtypeStruct((m, d), BF16),
        compiler_params=_params("parallel"),
        name="rmsnorm",
    )(x, g.reshape(1, d))


def _mm_kernel(x_ref, w_ref, o_ref):
    o_ref[...] = jnp.dot(x_ref[...], w_ref[...], preferred_element_type=F32).astype(o_ref.dtype)


def _mm_res_kernel(x_ref, w_ref, r_ref, o_ref):
    o_ref[...] = r_ref[...] + jnp.dot(x_ref[...], w_ref[...], preferred_element_type=F32)


def _out_proj_kernel(y1p_ref, y1s_ref, y2p_ref, y2s_ref, w1_ref, w2_ref, r_ref, o_ref, *, n_p):
    def go(y1_ref, y2_ref):
        acc = jnp.dot(y1_ref[...], w1_ref[...], preferred_element_type=F32)
        acc = acc + jnp.dot(y2_ref[...], w2_ref[...], preferred_element_type=F32)
        o_ref[...] = r_ref[...] + acc

    @pl.when(pl.program_id(0) < n_p)
    def _():
        go(y1p_ref, y2p_ref)

    @pl.when(pl.program_id(0) >= n_p)
    def _():
        go(y1s_ref, y2s_ref)


def _ffn_up_kernel(h_ref, wg_ref, wu_ref, o_ref):
    h = h_ref[...]
    a = jnp.dot(h, wg_ref[...], preferred_element_type=F32)
    b = jnp.dot(h, wu_ref[...], preferred_element_type=F32)
    o_ref[...] = (a * jax.nn.sigmoid(a) * b).astype(o_ref.dtype)


def matmul(x, w, *, tm_pref=1024, tn_pref=512, out_dtype=F32, res=None):
    m, k = x.shape
    n = w.shape[1]
    tm = _tile(m, tm_pref, 8)
    tn = _tile(n, tn_pref, LANE)
    in_specs = [pl.BlockSpec((tm, k), lambda i, j: (i, 0)),
                pl.BlockSpec((k, tn), lambda i, j: (0, j))]
    args = [x, w]
    body = _mm_kernel
    if res is not None:
        in_specs.append(pl.BlockSpec((tm, tn), lambda i, j: (i, j)))
        args.append(res)
        body = _mm_res_kernel
    return pl.pallas_call(
        body,
        grid=(m // tm, n // tn),
        in_specs=in_specs,
        out_specs=pl.BlockSpec((tm, tn), lambda i, j: (i, j)),
        out_shape=jax.ShapeDtypeStruct((m, n), out_dtype),
        compiler_params=_params("parallel", "arbitrary"),
        name="matmul",
    )(*args)


def _mm_res_ksplit_kernel(x_ref, w_ref, r_ref, o_ref):
    part = jnp.dot(x_ref[...], w_ref[...], preferred_element_type=F32)

    @pl.when(pl.program_id(2) == 0)
    def _():
        o_ref[...] = r_ref[...] + part

    @pl.when(pl.program_id(2) > 0)
    def _():
        o_ref[...] += part


def matmul_res_ksplit(x, w, layer, res, *, n_k=2, tm_pref=1024, tn_pref=512):
    m, k = x.shape
    n = w.shape[2]
    tm = _tile(m, tm_pref, 8)
    tn = _tile(n, tn_pref, LANE)
    assert k % (n_k * LANE) == 0
    tk = k // n_k
    return pl.pallas_call(
        _mm_res_ksplit_kernel,
        grid=(m // tm, n // tn, n_k),
        in_specs=[pl.BlockSpec((tm, tk), lambda i, j, kk: (i, kk)),
                  _layer_spec((tk, tn), lambda i, j, kk: (kk, j), layer),
                  pl.BlockSpec((tm, tn), lambda i, j, kk: (i, j))],
        out_specs=pl.BlockSpec((tm, tn), lambda i, j, kk: (i, j)),
        out_shape=jax.ShapeDtypeStruct((m, n), F32),
        compiler_params=_params("parallel", "arbitrary", "arbitrary"),
        name="matmul_ksplit",
    )(x, w, res)


def _mm_last_kernel(x_ref, w_ref, o_ref, b_ref):
    o_ref[...] = jnp.dot(x_ref[...], w_ref[...], preferred_element_type=F32)
    for s in range(b_ref.shape[0]):
        b_ref[s:s + 1, :] = o_ref[CHUNK * s + CHUNK - 1:CHUNK * (s + 1), :]


def matmul_with_block_ends(x, w, layer):
    m, k = x.shape
    n = w.shape[2]
    tm = _tile(m, 1024, 8 * CHUNK)
    tn = _tile(n, 512, LANE)
    return pl.pallas_call(
        _mm_last_kernel,
        grid=(m // tm, n // tn),
        in_specs=[pl.BlockSpec((tm, k), lambda i, j: (i, 0)),
                  _layer_spec((k, tn), lambda i, j: (0, j), layer)],
        out_specs=[pl.BlockSpec((tm, tn), lambda i, j: (i, j)),
                   pl.BlockSpec((tm // CHUNK, tn), lambda i, j: (i, j))],
        out_shape=[jax.ShapeDtypeStruct((m, n), F32), jax.ShapeDtypeStruct((m // CHUNK, n), F32)],
        compiler_params=_params("parallel", "arbitrary"),
        name="matmul_ends",
    )(x, w)


def out_proj(y1p, y1s, y2p, y2s, w, layer, res):
    mp, k1 = y1p.shape
    ms, k2 = y2s.shape
    n = w.shape[2]
    assert k1 == k2 and w.shape[1] == k1 + k2
    tm = _tile(math.gcd(mp, ms), 1024, 8)
    tn = _tile(n, 512, LANE)
    n_p = mp // tm
    row_p = lambda i, j: (jnp.minimum(i, n_p - 1), 0)
    row_s = lambda i, j: (jnp.maximum(i - n_p, 0), 0)
    return pl.pallas_call(
        functools.partial(_out_proj_kernel, n_p=n_p),
        grid=((mp + ms) // tm, n // tn),
        in_specs=[pl.BlockSpec((tm, k1), row_p), pl.BlockSpec((tm, k1), row_s),
                  pl.BlockSpec((tm, k2), row_p), pl.BlockSpec((tm, k2), row_s),
                  _layer_spec((k1, tn), lambda i, j: (0, j), layer),
                  _layer_spec((k2, tn), lambda i, j: (1, j), layer),
                  pl.BlockSpec((tm, tn), lambda i, j: (i, j))],
        out_specs=pl.BlockSpec((tm, tn), lambda i, j: (i, j)),
        out_shape=jax.ShapeDtypeStruct((mp + ms, n), F32),
        compiler_params=_params("parallel", "arbitrary"),
        name="out_proj",
    )(y1p, y1s, y2p, y2s, w, w, res)


def ffn_up(h, wg, wu, layer):
    m, k = h.shape
    n = wg.shape[2]
    tm = _tile(m, 1024, 8)
    tn = _tile(n, 256, LANE)
    return pl.pallas_call(
        _ffn_up_kernel,
        grid=(m // tm, n // tn),
        in_specs=[pl.BlockSpec((tm, k), lambda i, j: (i, 0)),
                  _layer_spec((k, tn), lambda i, j: (0, j), layer),
                  _layer_spec((k, tn), lambda i, j: (0, j), layer)],
        out_specs=pl.BlockSpec((tm, tn), lambda i, j: (i, j)),
        out_shape=jax.ShapeDtypeStruct((m, n), BF16),
        compiler_params=_params("parallel", "arbitrary"),
        name="ffn_up",
    )(h, wg, wu)


INV_BASE = 8


def _bf(x):
    return x.astype(BF16)


def _dot(a, b, dims=NN):
    return lax.dot_general(a, b, dims, preferred_element_type=F32)


def _head_sum(x, ones_bd):
    hi = _bf(x)
    lo = _bf(x - hi.astype(F32))
    return _dot(hi, ones_bd) + _dot(lo, ones_bd)


def _bd(x):
    lane = lax.broadcasted_iota(jnp.int32, x.shape, 1)
    zero = jnp.zeros_like(x)
    return jnp.concatenate([jnp.where(lane < RW_HD, x, zero), jnp.where(lane >= RW_HD, x, zero)], axis=0)


def _fold(z):
    lane = lax.broadcasted_iota(jnp.int32, (RW_HD, z.shape[1]), 1) % LANE
    return jnp.where(lane < RW_HD, z[:RW_HD], z[RW_HD:])


def _wkv_terms(r, lw, k, v, kkr, a, nb):
    n = CHUNK
    blk = nb * n
    rin = lax.broadcasted_iota(jnp.int32, (blk, LANE), 0) % n
    r2 = lax.broadcasted_iota(jnp.int32, (LANE, LANE), 0)
    c2 = lax.broadcasted_iota(jnp.int32, (LANE, LANE), 1)
    ones_bd = jnp.where((r2 // RW_HD) == (c2 // RW_HD), 1.0, 0.0).astype(BF16)
    rp = lax.broadcasted_iota(jnp.int32, (n, LANE), 0)
    cp = lax.broadcasted_iota(jnp.int32, (n, LANE), 1) % RW_HD
    strict = cp < rp
    incl = cp <= rp
    eye = cp == rp
    r4 = lax.broadcasted_iota(jnp.int32, (2 * n, LANE), 0)
    c4 = lax.broadcasted_iota(jnp.int32, (2 * n, LANE), 1) % RW_HD
    mask_ab = c4 < (r4 % n) + r4 // n

    kk = kkr * lax.rsqrt(jnp.maximum(_head_sum(kkr * kkr, ones_bd), 1e-24))
    b = kk * a
    cl = lw
    s = 1
    while s < n:
        cl = cl + jnp.where(rin >= s, pltpu.roll(cl, s, axis=0), 0.0)
        s *= 2
    ct = jnp.broadcast_to(cl.reshape(nb, n, LANE)[:, n - 1:n, :], (nb, n, LANE)).reshape(blk, LANE)
    kd = kk * jnp.exp(cl - lw)
    rd = r * jnp.exp(cl)
    einv = jnp.exp(-cl)
    ki = k * einv
    bi = b * einv
    e_end = jnp.exp(ct - cl)
    ki_e = k * e_end
    bi_e = b * e_end
    ect = jnp.exp(ct)

    rng = range(nb)
    sl = lambda x, c: x[c * n:(c + 1) * n]
    v_bd = [_bd(_bf(sl(v, c))) for c in rng]
    kr = [_bf(jnp.concatenate([sl(kd, c), sl(rd, c)], axis=0)) for c in rng]
    g1 = [jnp.where(mask_ab, _dot(kr[c], _bd(_bf(sl(bi, c))), NT), 0.0) for c in rng]
    g2 = [jnp.where(mask_ab, _dot(kr[c], _bd(_bf(sl(ki, c))), NT), 0.0) for c in rng]
    a_b = [g1[c][:n] for c in rng]

    blk_mask = lambda w: (rp // w) == (cp // w)
    nk = [jnp.where(blk_mask(INV_BASE), -a_b[c], 0.0) for c in rng]
    t = [jnp.where(eye, 1.0, 0.0) + nk[c] for c in rng]
    s = 2
    while s < INV_BASE:
        nk = [_dot(_bf(nk[c]), _bd(_bf(nk[c]))) for c in rng]
        t = [t[c] + _dot(_bf(t[c]), _bd(_bf(nk[c]))) for c in rng]
        s *= 2
    w = INV_BASE
    while w < n:
        off = blk_mask(2 * w) & jnp.logical_not(blk_mask(w))
        t_bf = [_bf(x) for x in t]
        lt = [_dot(_bf(jnp.where(off, a_b[c], 0.0)), _bd(t_bf[c])) for c in rng]
        t = [t[c] - _dot(t_bf[c], _bd(_bf(lt[c]))) for c in rng]
        w *= 2

    abkv = [_dot(_bf(g2[c]), v_bd[c]) for c in rng]
    x = [_bf(_dot(_bf(t[c]), jnp.concatenate([_bd(_bf(sl(kd, c))), _bd(_bf(abkv[c][:n]))], axis=1)))
         for c in rng]
    ry = [_dot(_bf(g1[c][n:]), jnp.concatenate([_bd(x[c][:, :LANE]), _bd(x[c][:, LANE:])], axis=1))
          for c in rng]
    bx = [_fold(_dot(_bf(sl(bi_e, c)), x[c], TN)) for c in rng]
    kv = [_fold(_dot(_bf(sl(ki_e, c)), _bf(sl(v, c)), TN)) for c in rng]
    m_c = [jnp.where(eye, ect[c * n:c * n + 1, :], 0.0) - bx[c][:, :LANE] for c in rng]
    n_c = [kv[c] - bx[c][:, LANE:] for c in rng]
    r_p = [sl(rd, c) - ry[c][:, :LANE] for c in rng]
    y_p = [abkv[c][n:] - ry[c][:, LANE:] for c in rng]
    rm = [_bf(jnp.concatenate([r_p[c], m_c[c]], axis=0)) for c in rng]
    return rm, n_c, y_p, ones_bd


def _wkv_epilogue(y, r, k, v, g, lnw, lnb, rk, ones_bd):
    inv_n = 1.0 / RW_HD
    mean = _head_sum(y, ones_bd) * inv_n
    yc = y - mean
    var = _head_sum(yc * yc, ones_bd) * inv_n
    yn = yc * lax.rsqrt(var + LNX_EPS) * lnw + lnb
    bonus = _head_sum(r * k * rk, ones_bd) * v
    return (yn + bonus) * g


def _rwkv_kernel(pr_ref, pk_ref, pv_ref, pl_ref, br_ref, bk_ref, bv_ref, bl_ref, mur_ref, muk_ref, muv_ref, mul_ref,
                 w0_ref, a0_ref, kk_ref, ka_ref, w2_ref, a2_ref, g2_ref, p0_ref, lnw_ref, lnb_ref, rk_ref,
                 y_ref, pt_ref, *scratch, nb, npair, carried, lora_tiles):
    if carried:
        p_scr, = scratch
        c_idx = pl.program_id(2)

        @pl.when(c_idx == 0)
        def _():
            for pp in range(npair):
                p_scr[pp] = p0_ref[0, pp]

    blk = nb * CHUNK
    run = blk if carried else CHUNK
    first = lax.broadcasted_iota(jnp.int32, (run, 1), 0) == 0

    def shifted(p_ref, b_ref, mu_ref):
        mu = mu_ref[...]
        outs = []
        for s in range(blk // run):
            sub = p_ref[run * s:run * (s + 1), :]
            prev = jnp.where(first, b_ref[s:s + 1, :], pltpu.roll(sub, 1, axis=0))
            outs.append(sub + (prev - sub) * mu)
        return outs[0] if len(outs) == 1 else jnp.concatenate(outs, axis=0)

    r = shifted(pr_ref, br_ref, mur_ref)
    xk = shifted(pk_ref, bk_ref, muk_ref)
    v = shifted(pv_ref, bv_ref, muv_ref)
    xl = shifted(pl_ref, bl_ref, mul_ref)
    (w_lo, w_hi), (a_lo, a_hi), (g_lo, g_hi) = lora_tiles
    dw = jnp.dot(jnp.tanh(xl[:, w_lo:w_hi]).astype(BF16), w2_ref[w_lo:w_hi, :], preferred_element_type=F32)
    da = jnp.dot(xl[:, a_lo:a_hi].astype(BF16), a2_ref[a_lo:a_hi, :], preferred_element_type=F32)
    g = jnp.dot(jax.nn.sigmoid(xl[:, g_lo:g_hi]).astype(BF16), g2_ref[g_lo:g_hi, :], preferred_element_type=F32)
    w_log = -jax.nn.softplus(-(w0_ref[...] + dw)) - 0.5
    lw = -jnp.exp(w_log)
    a = jax.nn.sigmoid(a0_ref[...] + da)
    k = xk * (1.0 + (a - 1.0) * ka_ref[...])
    kkr = xk * kk_ref[...]

    terms = []
    for pp in range(npair):
        ln = slice(pp * LANE, (pp + 1) * LANE)
        terms.append(_wkv_terms(r[:, ln], lw[:, ln], k[:, ln], v[:, ln], kkr[:, ln], a[:, ln], nb))
    p = [p_scr[pp] for pp in range(npair)] if carried else None
    ys = [[] for _ in range(npair)]
    for c in range(nb):
        for pp in range(npair):
            rm, n_c, y_p, _ = terms[pp]
            yp = _dot(rm[c], _bd(_bf(p[pp] if carried else p0_ref[c, pp])))
            ys[pp].append(yp[:CHUNK] + y_p[c])
            p_new = yp[CHUNK:] + n_c[c]
            if carried:
                p[pp] = p_new
            else:
                pt_ref[c, pp] = p_new
    for pp in range(npair):
        ln = slice(pp * LANE, (pp + 1) * LANE)
        y = jnp.concatenate(ys[pp], axis=0) if nb > 1 else ys[pp][0]
        y_ref[:, ln] = _wkv_epilogue(y, r[:, ln], k[:, ln], v[:, ln], g[:, ln], lnw_ref[:, ln],
                                     lnb_ref[:, ln], rk_ref[:, ln], terms[pp][3]).astype(y_ref.dtype)
    if carried:
        for pp in range(npair):
            p_scr[pp] = p[pp]

        @pl.when(c_idx == pl.num_programs(2) - 1)
        def _():
            for pp in range(npair):
                pt_ref[0, pp] = p[pp]


def rwkv_mix(p_rw, last_rows, shift0, mu, w0, a0, k_k, k_a, w2p, a2p, g2p, layer, pt0, lnw, lnb, rk, *,
             row0, n_seq, t_seq, rw_w, lora_w, lora_dims, nb_pref=16, npair_pref=4):
    carried = t_seq > CHUNK
    cols = p_rw.shape[1]
    n_pairs = rw_w // LANE
    npair = _tile(n_pairs, npair_pref, 1)
    gw = npair * LANE
    ncb = rw_w // gw
    lora_blk = (3 * rw_w) // lora_w
    if carried:
        blk = _tile(t_seq, nb_pref * CHUNK, CHUNK)
        nb, n_blk = blk // CHUNK, t_seq // blk
        grid = (n_seq, ncb, n_blk)
        step = lambda *g: g[0] * n_blk + g[2]
        ends = last_rows.reshape(n_seq, t_seq // CHUNK, cols)[:, nb - 1::nb]
        bnd = jnp.concatenate([shift0[:, None], ends[:, :-1]], axis=1).reshape(n_seq * n_blk, 1, cols)
        st = pl.BlockSpec((1, npair, RW_HD, LANE), lambda *g: (g[0], g[1], 0, 0))
        sem = ("parallel", "parallel", "arbitrary")
        scratch = [pltpu.VMEM((npair, RW_HD, LANE), F32)]
    else:
        nb = _tile(n_seq, nb_pref, 1)
        blk = nb * CHUNK
        grid = (n_seq // nb, ncb)
        step = lambda *g: g[0]
        bnd = shift0.reshape(n_seq // nb, nb, cols)
        st = pl.BlockSpec((nb, npair, RW_HD, LANE), lambda *g: (g[0], g[1], 0, 0))
        sem = ("parallel", "parallel")
        scratch = []
    base = row0 // blk
    nbnd = bnd.shape[1]
    tile = lambda lo, hi: (lo // LANE * LANE, -(-hi // LANE) * LANE)
    w_l, a_l, g_l = lora_dims
    lora_tiles = (tile(0, w_l), tile(w_l, w_l + a_l), tile(w_l + a_l, w_l + a_l + g_l))

    def tok(sec):
        return pl.BlockSpec((blk, gw), lambda *g: (base + step(*g), sec * ncb + g[1]))

    def bound(sec):
        return pl.BlockSpec((None, nbnd, gw), lambda *g: (step(*g), 0, sec * ncb + g[1]))

    def mus(sec):
        return pl.BlockSpec((1, gw), lambda *g: (0, sec * ncb + g[1]))

    vec = pl.BlockSpec((1, gw), lambda *g: (0, g[1]))
    wsp = _layer_spec((lora_w, gw), lambda *g: (0, g[1]), layer)
    in_specs = ([tok(0), tok(1), tok(2), pl.BlockSpec((blk, lora_w), lambda *g: (base + step(*g), lora_blk))]
                + [bound(0), bound(1), bound(2),
                   pl.BlockSpec((None, nbnd, lora_w), lambda *g: (step(*g), 0, lora_blk))]
                + [mus(0), mus(1), mus(2), pl.BlockSpec((1, lora_w), lambda *g: (0, lora_blk))]
                + [vec] * 4 + [wsp] * 3 + [st, vec, vec, vec])
    return pl.pallas_call(
        functools.partial(_rwkv_kernel, nb=nb, npair=npair, carried=carried, lora_tiles=lora_tiles),
        grid=grid,
        in_specs=in_specs,
        out_specs=[pl.BlockSpec((blk, gw), lambda *g: (step(*g), g[1])), st],
        out_shape=[jax.ShapeDtypeStruct((n_seq * t_seq, rw_w), BF16), jax.ShapeDtypeStruct(pt0.shape, F32)],
        scratch_shapes=scratch,
        compiler_params=_params(*sem),
        name="rwkv_seq" if carried else "rwkv_par",
    )(p_rw, p_rw, p_rw, p_rw, bnd, bnd, bnd, bnd, mu, mu, mu, mu, w0, a0, k_k, k_a, w2p, a2p, g2p,
      pt0, lnw, lnb, rk)


def _rope(x, cos2, sin2):
    half = QK_ROPE // 2
    rot = jnp.concatenate([-x[:, half:], x[:, :half]], axis=1)
    return x * cos2 + rot * sin2


def _q_prep_kernel(pq_ref, g_ref, w_ref, cos_ref, sin_ref, gn_ref, gr_ref, o_ref, *, n_heads):
    x = pq_ref[...]
    ms = jnp.mean(x * x, axis=-1, keepdims=True)
    xn = (x * lax.rsqrt(ms + NORM_EPS) * g_ref[...]).astype(BF16)
    cos2 = cos_ref[...]
    sin2 = sin_ref[...]
    gn = gn_ref[...]
    gr = gr_ref[...]
    for h in range(n_heads):
        q = jnp.dot(xn, w_ref[h], preferred_element_type=F32)
        nope = q[:, :QK_NOPE]
        rp = _rope(q[:, QK_NOPE:], cos2, sin2)
        ss = jnp.sum(nope * nope, axis=-1, keepdims=True) + jnp.sum(rp * rp, axis=-1, keepdims=True)
        scale = lax.rsqrt(ss * (1.0 / QK_DIM) + NORM_EPS) * QK_SCALE
        o_ref[h] = jnp.concatenate([nope * gn * scale, rp * gr * scale], axis=1).astype(o_ref.dtype)


def q_prep(p_q, g_qa, w_uq_h, cos2, sin2, gn, gr):
    m, ql = p_q.shape
    n_heads = w_uq_h.shape[0]
    tm = _tile(m, 256, 16)
    return pl.pallas_call(
        functools.partial(_q_prep_kernel, n_heads=n_heads),
        grid=(m // tm,),
        in_specs=[pl.BlockSpec((tm, ql), lambda i: (i, 0)),
                  pl.BlockSpec((1, ql), lambda i: (0, 0)),
                  pl.BlockSpec((n_heads, ql, QK_DIM), lambda i: (0, 0, 0)),
                  pl.BlockSpec((tm, QK_ROPE), lambda i: (i, 0)),
                  pl.BlockSpec((tm, QK_ROPE), lambda i: (i, 0)),
                  pl.BlockSpec((1, QK_NOPE), lambda i: (0, 0)),
                  pl.BlockSpec((1, QK_ROPE), lambda i: (0, 0))],
        out_specs=pl.BlockSpec((n_heads, tm, QK_DIM), lambda i: (0, i, 0)),
        out_shape=jax.ShapeDtypeStruct((n_heads, m, QK_DIM), BF16),
        compiler_params=_params("parallel"),
        name="q_prep",
    )(p_q, g_qa, w_uq_h, cos2, sin2, gn, gr)


def _kv_prep_kernel(p_ref, g_ref, cos_ref, sin_ref, ckv_ref, kr_ref, *, kv_lora):
    x = p_ref[:, :kv_lora]
    ms = jnp.mean(x * x, axis=-1, keepdims=True)
    ckv_ref[...] = x * lax.rsqrt(ms + NORM_EPS) * g_ref[...]
    kr_ref[...] = _rope(p_ref[:, kv_lora:kv_lora + QK_ROPE], cos_ref[...], sin_ref[...])


def kv_prep(p_kvr, g_kva, cos2, sin2, kv_lora):
    m, wd = p_kvr.shape
    tm = _tile(m, 512, 8)
    return pl.pallas_call(
        functools.partial(_kv_prep_kernel, kv_lora=kv_lora),
        grid=(m // tm,),
        in_specs=[pl.BlockSpec((tm, wd), lambda i: (i, 0)),
                  pl.BlockSpec((1, kv_lora), lambda i: (0, 0)),
                  pl.BlockSpec((tm, QK_ROPE), lambda i: (i, 0)),
                  pl.BlockSpec((tm, QK_ROPE), lambda i: (i, 0))],
        out_specs=[pl.BlockSpec((tm, kv_lora), lambda i: (i, 0)),
                   pl.BlockSpec((tm, QK_ROPE), lambda i: (i, 0))],
        out_shape=[jax.ShapeDtypeStruct((m, kv_lora), F32),
                   jax.ShapeDtypeStruct((m, QK_ROPE), F32)],
        compiler_params=_params("parallel"),
        name="kv_prep",
    )(p_kvr, g_kva, cos2, sin2)


def _kv_expand_kernel(c_ref, kr_ref, w_ref, gn_ref, gr_ref, k_ref, v_ref, *, hb):
    hw = QK_NOPE + V_HD
    kv = jnp.dot(c_ref[...].astype(BF16), w_ref[...], preferred_element_type=F32)
    kr = kr_ref[...]
    ss_r = jnp.sum(kr * kr, axis=-1, keepdims=True)
    krg = kr * gr_ref[...]
    for h in range(hb):
        kn = kv[:, h * hw:h * hw + QK_NOPE]
        rstd = lax.rsqrt((jnp.sum(kn * kn, axis=-1, keepdims=True) + ss_r) * (1.0 / QK_DIM) + NORM_EPS)
        k_ref[h] = jnp.concatenate([kn * gn_ref[...] * rstd, krg * rstd], axis=1).astype(k_ref.dtype)
        v_ref[h] = kv[:, h * hw + QK_NOPE:(h + 1) * hw].astype(v_ref.dtype)


def kv_expand(ckv, krope, w_ukv, layer, gn, gr, n_heads, rows=None):
    kv_lora = ckv.shape[1]
    rws = ckv.shape[0] if rows is None else rows
    tr = _tile(rws, 1024, 16)
    hb = _tile(n_heads, 4, 1)
    hw = QK_NOPE + V_HD
    return pl.pallas_call(
        functools.partial(_kv_expand_kernel, hb=hb),
        grid=(rws // tr, n_heads // hb),
        in_specs=[pl.BlockSpec((tr, kv_lora), lambda i, h: (i, 0)),
                  pl.BlockSpec((tr, QK_ROPE), lambda i, h: (i, 0)),
                  _layer_spec((kv_lora, hb * hw), lambda i, h: (0, h), layer),
                  pl.BlockSpec((1, QK_NOPE), lambda i, h: (0, 0)),
                  pl.BlockSpec((1, QK_ROPE), lambda i, h: (0, 0))],
        out_specs=[pl.BlockSpec((hb, tr, QK_DIM), lambda i, h: (h, i, 0)),
                   pl.BlockSpec((hb, tr, V_HD), lambda i, h: (h, i, 0))],
        out_shape=[jax.ShapeDtypeStruct((n_heads, rws, QK_DIM), BF16),
                   jax.ShapeDtypeStruct((n_heads, rws, V_HD), BF16)],
        compiler_params=_params("parallel", "arbitrary"),
        name="kv_expand",
    )(ckv, krope, w_ukv, gn, gr)


def _attn_prompt_kernel(q_ref, k_ref, v_ref, o_ref, *, blk):
    i = pl.program_id(2)
    q = q_ref[0]

    def step(kb, vb, carry, mask):
        m, l, acc = carry
        s = lax.dot_general(q, kb, NT, preferred_element_type=F32)
        if mask is not None:
            s = jnp.where(mask, s, -1e30)
        m_new = jnp.maximum(m, jnp.max(s, axis=-1, keepdims=True))
        p = jnp.exp(s - m_new)
        alpha = jnp.exp(m - m_new)
        l = alpha * l + jnp.sum(p, axis=-1, keepdims=True)
        acc = alpha * acc + jnp.dot(p.astype(BF16), vb, preferred_element_type=F32)
        return m_new, l, acc

    def body(j, carry):
        off = pl.multiple_of(j * blk, blk)
        return step(k_ref[0, pl.ds(off, blk), :], v_ref[0, pl.ds(off, blk), :], carry, None)

    init = (jnp.full((blk, 1), -1e30, F32), jnp.zeros((blk, 1), F32), jnp.zeros((blk, V_HD), F32))
    carry = lax.fori_loop(0, i, body, init)
    qc = lax.broadcasted_iota(jnp.int32, (blk, blk), 0) // CHUNK
    kc = lax.broadcasted_iota(jnp.int32, (blk, blk), 1) // CHUNK
    off = pl.multiple_of(i * blk, blk)
    m, l, acc = step(k_ref[0, pl.ds(off, blk), :], v_ref[0, pl.ds(off, blk), :], carry, kc <= qc)
    o_ref[...] = (acc / l).astype(o_ref.dtype)


def attn_prompt(q, k, v, *, n_seq, t_seq):
    n_heads = q.shape[0]
    blk = _tile(t_seq, 1024, CHUNK)
    nq = t_seq // blk
    return pl.pallas_call(
        functools.partial(_attn_prompt_kernel, blk=blk),
        grid=(n_seq, n_heads, nq),
        in_specs=[pl.BlockSpec((1, blk, QK_DIM), lambda b, h, i: (h, b * nq + i, 0)),
                  pl.BlockSpec((1, t_seq, QK_DIM), lambda b, h, i: (h, b, 0)),
                  pl.BlockSpec((1, t_seq, V_HD), lambda b, h, i: (h, b, 0))],
        out_specs=pl.BlockSpec((blk, V_HD), lambda b, h, i: (b * nq + i, h)),
        out_shape=jax.ShapeDtypeStruct((n_seq * t_seq, n_heads * V_HD), BF16),
        compiler_params=_params("parallel", "parallel", "arbitrary"),
        name="attn_prompt",
    )(q, k, v)


def _attn_sample_kernel(q_ref, k_ref, v_ref, o_ref, *, hb):
    for h in range(hb):
        s = lax.dot_general(q_ref[h], k_ref[h], NT, preferred_element_type=F32)
        m = jnp.max(s, axis=-1, keepdims=True)
        p = jnp.exp(s - m)
        l = jnp.sum(p, axis=-1, keepdims=True)
        acc = jnp.dot(p.astype(BF16), v_ref[h], preferred_element_type=F32)
        o_ref[:, h * V_HD:(h + 1) * V_HD] = (acc / l).astype(o_ref.dtype)


def attn_sample(q, k, v, *, row0, n_seq, t_q, t_kv):
    n_heads = q.shape[0]
    hb = _tile(n_heads, 4, 1)
    base = row0 // t_q
    return pl.pallas_call(
        functools.partial(_attn_sample_kernel, hb=hb),
        grid=(n_seq, n_heads // hb),
        in_specs=[pl.BlockSpec((hb, t_q, QK_DIM), lambda b, h: (h, base + b, 0)),
                  pl.BlockSpec((hb, t_kv, QK_DIM), lambda b, h: (h, b, 0)),
                  pl.BlockSpec((hb, t_kv, V_HD), lambda b, h: (h, b, 0))],
        out_specs=pl.BlockSpec((t_q, hb * V_HD), lambda b, h: (b, h)),
        out_shape=jax.ShapeDtypeStruct((n_seq * t_q, n_heads * V_HD), BF16),
        compiler_params=_params("parallel", "parallel"),
        name="attn_sample",
    )(q, k, v)


def _rope_tables(positions):
    half = QK_ROPE // 2
    inv_freq = ROPE_THETA ** (-jnp.arange(half, dtype=F32) / half)
    ang = positions.astype(F32)[:, None] * inv_freq[None, :]
    cos, sin = jnp.cos(ang), jnp.sin(ang)
    return jnp.concatenate([cos, cos], axis=1), jnp.concatenate([sin, sin], axis=1)


def _pad_cols(w, n):
    return jnp.pad(w, [(0, 0)] * (w.ndim - 1) + [(0, n - w.shape[-1])])


def kernel(x_prompt, x_sample, cache_ckv, cache_krope, state_wkv, state_shift, norm_attn_g, w_in, rw_mu, rw_w0, rw_w2, rw_a0, rw_a2, rw_g2, rw_k_k, rw_k_a, rw_r_k, rw_lnx_w, rw_lnx_b, mla_q_a_g, mla_w_uq, mla_kv_a_g, mla_w_ukv, mla_q_norm_g, mla_k_norm_g, w_out, norm_ffn_g, w_gate, w_up, w_down):
    bp, tp, d = x_prompt.shape
    bs, ts, _ = x_sample.shape
    depth = w_in.shape[0]
    past = cache_ckv.shape[2]
    kv_lora = cache_ckv.shape[3]
    rw_h = rw_r_k.shape[1]
    rw_w = rw_h * RW_HD
    w_lora, a_lora, g_lora = rw_w2.shape[1], rw_a2.shape[1], rw_g2.shape[1]
    lora = w_lora + a_lora + g_lora
    lora_w = -(-lora // LANE) * LANE
    rw_cols = 3 * rw_w + lora
    rw_pad = 3 * rw_w + lora_w
    q_lora = mla_w_uq.shape[1]
    mla_h = mla_w_uq.shape[2] // QK_DIM
    assert (3 * rw_w) % lora_w == 0 and tp % CHUNK == 0 and ts == CHUNK
    mp, ms = bp * tp, bs * ts
    m = mp + ms
    kvr_w = -(-(kv_lora + QK_ROPE) // LANE) * LANE

    layers = range(depth)
    w_rw = _pad_cols(w_in[:, :, :rw_cols], rw_pad).astype(BF16)
    w_q = [w_in[l, :, rw_cols:rw_cols + q_lora].astype(BF16) for l in layers]
    w_kvr = [_pad_cols(w_in[l, :, rw_cols + q_lora:], kvr_w).astype(BF16) for l in layers]
    mu_p = _pad_cols(rw_mu, rw_pad).reshape(depth, 1, rw_pad)
    zl = lambda w, r0: jnp.zeros((depth, lora_w, rw_w), F32).at[:, r0:r0 + w.shape[1]].set(w).astype(BF16)
    w2p, a2p, g2p = zl(rw_w2, 0), zl(rw_a2, w_lora), zl(rw_g2, w_lora + a_lora)
    vec = lambda t: t.reshape(depth, 1, -1)
    w_uq_h = [mla_w_uq[l].reshape(q_lora, mla_h, QK_DIM).transpose(1, 0, 2).astype(BF16) for l in layers]
    w_ukv, w_o = mla_w_ukv.astype(BF16), w_out.astype(BF16)
    w_g, w_u, w_d = w_gate.astype(BF16), w_up.astype(BF16), w_down.astype(BF16)
    qn_n, qn_r = mla_q_norm_g[:, None, :QK_NOPE], jnp.tile(mla_q_norm_g[:, None, QK_NOPE:], (1, 1, 2))
    kn_n, kn_r = mla_k_norm_g[:, None, :QK_NOPE], jnp.tile(mla_k_norm_g[:, None, QK_NOPE:], (1, 1, 2))

    pos = jnp.concatenate([jnp.tile(jnp.arange(tp), bp), jnp.tile(past + jnp.arange(ts), bs)])
    cos2, sin2 = _rope_tables(pos)

    x = jnp.concatenate([x_prompt.reshape(mp, d), x_sample.reshape(ms, d)], axis=0)
    zero_shift = jnp.zeros((bp, rw_pad), F32)
    to_pairs = lambda s: (s.reshape(-1, rw_h // 2, 2, RW_HD, RW_HD).transpose(0, 1, 4, 2, 3)
                          .reshape(-1, rw_h // 2, RW_HD, 2 * RW_HD))
    from_pairs = lambda p: (p.reshape(-1, rw_h // 2, RW_HD, 2, RW_HD).transpose(0, 1, 3, 4, 2)
                            .reshape(-1, rw_h, RW_HD, RW_HD))
    zero_state = jnp.zeros((bp, rw_h // 2, RW_HD, 2 * RW_HD), F32)
    outs = [[] for _ in range(8)]

    for l in range(depth):
        h = rmsnorm_bf16(x, norm_attn_g[l])
        p_rw, last = matmul_with_block_ends(h, w_rw, l)
        p_q = matmul(h, w_q[l], tn_pref=q_lora)
        p_kvr = matmul(h, w_kvr[l], tn_pref=kvr_w)

        lnw, lnb, rk = vec(rw_lnx_w)[l], vec(rw_lnx_b)[l], rw_r_k[l].reshape(1, rw_w)
        rw_args = (mu_p[l], vec(rw_w0)[l], vec(rw_a0)[l], vec(rw_k_k)[l], vec(rw_k_a)[l], w2p, a2p, g2p, l)
        rw_dims = dict(rw_w=rw_w, lora_w=lora_w, lora_dims=(w_lora, a_lora, g_lora))
        y_rw_p, pst_p = rwkv_mix(p_rw, last[:mp // CHUNK], zero_shift, *rw_args, zero_state, lnw, lnb, rk,
                                 row0=0, n_seq=bp, t_seq=tp, **rw_dims)
        y_rw_s, pst_s = rwkv_mix(p_rw, last[mp // CHUNK:], _pad_cols(state_shift[l], rw_pad), *rw_args,
                                 to_pairs(state_wkv[l]), lnw, lnb, rk, row0=mp, n_seq=bs, t_seq=ts, **rw_dims)

        q = q_prep(p_q, vec(mla_q_a_g)[l], w_uq_h[l], cos2, sin2, qn_n[l], qn_r[l])
        ckv_new, kr_new = kv_prep(p_kvr, vec(mla_kv_a_g)[l], cos2, sin2, kv_lora)
        k_p, v_p = kv_expand(ckv_new, kr_new, w_ukv, l, kn_n[l], kn_r[l], mla_h, rows=mp)
        y_mla_p = attn_prompt(q, k_p, v_p, n_seq=bp, t_seq=tp)
        ckv_s = jnp.concatenate([cache_ckv[l], ckv_new[mp:].reshape(bs, ts, kv_lora)], axis=1)
        kr_s = jnp.concatenate([cache_krope[l], kr_new[mp:].reshape(bs, ts, QK_ROPE)], axis=1)
        t_kv = past + ts
        k_s, v_s = kv_expand(ckv_s.reshape(bs * t_kv, kv_lora), kr_s.reshape(bs * t_kv, QK_ROPE),
                             w_ukv, l, kn_n[l], kn_r[l], mla_h)
        y_mla_s = attn_sample(q, k_s, v_s, row0=mp, n_seq=bs, t_q=ts, t_kv=t_kv)

        x = out_proj(y_rw_p, y_rw_s, y_mla_p, y_mla_s, w_o, l, x)

        h2 = rmsnorm_bf16(x, norm_ffn_g[l])
        u = ffn_up(h2, w_g, w_u, l)
        x = matmul_res_ksplit(u, w_d, l, x)

        outs[0].append(ckv_new[:mp].reshape(bp, tp, kv_lora))
        outs[1].append(kr_new[:mp].reshape(bp, tp, QK_ROPE))
        outs[2].append(from_pairs(pst_p))
        outs[3].append(p_rw[tp - 1:mp:tp, :rw_cols])
        outs[4].append(ckv_new[mp:].reshape(bs, ts, kv_lora))
        outs[5].append(kr_new[mp:].reshape(bs, ts, QK_ROPE))
        outs[6].append(from_pairs(pst_s))
        outs[7].append(p_rw[mp + ts - 1::ts, :rw_cols])

    return (x[:mp].reshape(bp, tp, d), x[mp:].reshape(bs, ts, d)) + tuple(jnp.stack(o) for o in outs)
```

```python
import functools
import math

import jax
import jax.numpy as jnp
from jax import lax
from jax.experimental import pallas as pl
from jax.experimental.pallas import tpu as pltpu

F32 = jnp.float32
BF16 = jnp.bfloat16

CHUNK = 64
RW_HD = 64
QK_NOPE = 128
QK_ROPE = 64
QK_DIM = QK_NOPE + QK_ROPE
V_HD = 128
ROPE_THETA = 10000.0
NORM_EPS = 1e-6
LNX_EPS = 64e-5
QK_SCALE = QK_DIM ** -0.5

LANE = 128
VMEM_LIMIT = 56 * 1024 * 1024

NN = (((1,), (0,)), ((), ()))
NT = (((1,), (1,)), ((), ()))
TN = (((0,), (0,)), ((), ()))


def _tile(n, pref, quantum):
    if n <= pref:
        return n
    t = (pref // quantum) * quantum
    while t >= quantum:
        if n % t == 0:
            return t
        t -= quantum
    raise ValueError(f"no tile for {n} with quantum {quantum}")


def _layer_spec(block, index, layer):
    return pl.BlockSpec((None,) + block, lambda *g: (layer,) + index(*g))


def _params(*sem):
    return pltpu.CompilerParams(dimension_semantics=sem, vmem_limit_bytes=VMEM_LIMIT)


def _rmsnorm_kernel(x_ref, g_ref, o_ref):
    x = x_ref[...]
    ms = jnp.mean(x * x, axis=-1, keepdims=True)
    o_ref[...] = (x * lax.rsqrt(ms + NORM_EPS) * g_ref[...]).astype(o_ref.dtype)


def rmsnorm_bf16(x, g):
    m, d = x.shape
    tm = _tile(m, 256, 8)
    return pl.pallas_call(
        _rmsnorm_kernel,
        grid=(m // tm,),
        in_specs=[pl.BlockSpec((tm, d), lambda i: (i, 0)),
                  pl.BlockSpec((1, d), lambda i: (0, 0))],
        out_specs=pl.BlockSpec((tm, d), lambda i: (i, 0)),
        out_shape=jax.ShapeDtypeStruct((m, d), BF16),
        compiler_params=_params("parallel"),
        name="rmsnorm",
    )(x, g.reshape(1, d))


def _mm_kernel(x_ref, w_ref, o_ref):
    o_ref[...] = jnp.dot(x_ref[...], w_ref[...], preferred_element_type=F32).astype(o_ref.dtype)


def _mm_res_kernel(x_ref, w_ref, r_ref, o_ref):
    o_ref[...] = r_ref[...] + jnp.dot(x_ref[...], w_ref[...], preferred_element_type=F32)


def _out_proj_kernel(y1p_ref, y1s_ref, y2p_ref, y2s_ref, w1_ref, w2_ref, r_ref, o_ref, *, n_p):
    def go(y1_ref, y2_ref):
        acc = jnp.dot(y1_ref[...], w1_ref[...], preferred_element_type=F32)
        acc = acc + jnp.dot(y2_ref[...], w2_ref[...], preferred_element_type=F32)
        o_ref[...] = r_ref[...] + acc

    @pl.when(pl.program_id(0) < n_p)
    def _():
        go(y1p_ref, y2p_ref)

    @pl.when(pl.program_id(0) >= n_p)
    def _():
        go(y1s_ref, y2s_ref)


def _ffn_up_kernel(h_ref, wg_ref, wu_ref, o_ref):
    h = h_ref[...]
    a = jnp.dot(h, wg_ref[...], preferred_element_type=F32)
    b = jnp.dot(h, wu_ref[...], preferred_element_type=F32)
    o_ref[...] = (a * jax.nn.sigmoid(a) * b).astype(o_ref.dtype)


def matmul(x, w, *, tm_pref=1024, tn_pref=512, out_dtype=F32, res=None):
    m, k = x.shape
    n = w.shape[1]
    tm = _tile(m, tm_pref, 8)
    tn = _tile(n, tn_pref, LANE)
    in_specs = [pl.BlockSpec((tm, k), lambda i, j: (i, 0)),
                pl.BlockSpec((k, tn), lambda i, j: (0, j))]
    args = [x, w]
    body = _mm_kernel
    if res is not None:
        in_specs.append(pl.BlockSpec((tm, tn), lambda i, j: (i, j)))
        args.append(res)
        body = _mm_res_kernel
    return pl.pallas_call(
        body,
        grid=(m // tm, n // tn),
        in_specs=in_specs,
        out_specs=pl.BlockSpec((tm, tn), lambda i, j: (i, j)),
        out_shape=jax.ShapeDtypeStruct((m, n), out_dtype),
        compiler_params=_params("parallel", "arbitrary"),
        name="matmul",
    )(*args)


def _mm_res_ksplit_kernel(x_ref, w_ref, r_ref, o_ref):
    part = jnp.dot(x_ref[...], w_ref[...], preferred_element_type=F32)

    @pl.when(pl.program_id(2) == 0)
    def _():
        o_ref[...] = r_ref[...] + part

    @pl.when(pl.program_id(2) > 0)
    def _():
        o_ref[...] += part


def matmul_res_ksplit(x, w, layer, res, *, n_k=2, tm_pref=1024, tn_pref=512):
    m, k = x.shape
    n = w.shape[2]
    tm = _tile(m, tm_pref, 8)
    tn = _tile(n, tn_pref, LANE)
    assert k % (n_k * LANE) == 0
    tk = k // n_k
    return pl.pallas_call(
        _mm_res_ksplit_kernel,
        grid=(m // tm, n // tn, n_k),
        in_specs=[pl.BlockSpec((tm, tk), lambda i, j, kk: (i, kk)),
                  _layer_spec((tk, tn), lambda i, j, kk: (kk, j), layer),
                  pl.BlockSpec((tm, tn), lambda i, j, kk: (i, j))],
        out_specs=pl.BlockSpec((tm, tn), lambda i, j, kk: (i, j)),
        out_shape=jax.ShapeDtypeStruct((m, n), F32),
        compiler_params=_params("parallel", "arbitrary", "arbitrary"),
        name="matmul_ksplit",
    )(x, w, res)


def _mm_last_kernel(x_ref, w_ref, o_ref, b_ref):
    o_ref[...] = jnp.dot(x_ref[...], w_ref[...], preferred_element_type=F32)
    for s in range(b_ref.shape[0]):
        b_ref[s:s + 1, :] = o_ref[CHUNK * s + CHUNK - 1:CHUNK * (s + 1), :]


def matmul_with_block_ends(x, w, layer):
    m, k = x.shape
    n = w.shape[2]
    tm = _tile(m, 1024, 8 * CHUNK)
    tn = _tile(n, 512, LANE)
    return pl.pallas_call(
        _mm_last_kernel,
        grid=(m // tm, n // tn),
        in_specs=[pl.BlockSpec((tm, k), lambda i, j: (i, 0)),
                  _layer_spec((k, tn), lambda i, j: (0, j), layer)],
        out_specs=[pl.BlockSpec((tm, tn), lambda i, j: (i, j)),
                   pl.BlockSpec((tm // CHUNK, tn), lambda i, j: (i, j))],
        out_shape=[jax.ShapeDtypeStruct((m, n), F32), jax.ShapeDtypeStruct((m // CHUNK, n), F32)],
        compiler_params=_params("parallel", "arbitrary"),
        name="matmul_ends",
    )(x, w)


def out_proj(y1p, y1s, y2p, y2s, w, layer, res):
    mp, k1 = y1p.shape
    ms, k2 = y2s.shape
    n = w.shape[2]
    assert k1 == k2 and w.shape[1] == k1 + k2
    tm = _tile(math.gcd(mp, ms), 1024, 8)
    tn = _tile(n, 512, LANE)
    n_p = mp // tm
    row_p = lambda i, j: (jnp.minimum(i, n_p - 1), 0)
    row_s = lambda i, j: (jnp.maximum(i - n_p, 0), 0)
    return pl.pallas_call(
        functools.partial(_out_proj_kernel, n_p=n_p),
        grid=((mp + ms) // tm, n // tn),
        in_specs=[pl.BlockSpec((tm, k1), row_p), pl.BlockSpec((tm, k1), row_s),
                  pl.BlockSpec((tm, k2), row_p), pl.BlockSpec((tm, k2), row_s),
                  _layer_spec((k1, tn), lambda i, j: (0, j), layer),
                  _layer_spec((k2, tn), lambda i, j: (1, j), layer),
                  pl.BlockSpec((tm, tn), lambda i, j: (i, j))],
        out_specs=pl.BlockSpec((tm, tn), lambda i, j: (i, j)),
        out_shape=jax.ShapeDtypeStruct((mp + ms, n), F32),
        compiler_params=_params("parallel", "arbitrary"),
        name="out_proj",
    )(y1p, y1s, y2p, y2s, w, w, res)


def ffn_up(h, wg, wu, layer):
    m, k = h.shape
    n = wg.shape[2]
    tm = _tile(m, 2304, 16)
    tn = _tile(n, 256, LANE)
    return pl.pallas_call(
        _ffn_up_kernel,
        grid=(m // tm, n // tn),
        in_specs=[pl.BlockSpec((tm, k), lambda i, j: (i, 0)),
                  _layer_spec((k, tn), lambda i, j: (0, j), layer),
                  _layer_spec((k, tn), lambda i, j: (0, j), layer)],
        out_specs=pl.BlockSpec((tm, tn), lambda i, j: (i, j)),
        out_shape=jax.ShapeDtypeStruct((m, n), BF16),
        compiler_params=_params("parallel", "arbitrary"),
        name="ffn_up",
    )(h, wg, wu)


INV_BASE = 8


def _bf(x):
    return x.astype(BF16)


def _dot(a, b, dims=NN):
    return lax.dot_general(a, b, dims, preferred_element_type=F32)


def _head_sum(x, ones_bd):
    hi = _bf(x)
    lo = _bf(x - hi.astype(F32))
    return _dot(hi, ones_bd) + _dot(lo, ones_bd)


def _bd(x):
    lane = lax.broadcasted_iota(jnp.int32, x.shape, 1)
    zero = jnp.zeros_like(x)
    return jnp.concatenate([jnp.where(lane < RW_HD, x, zero), jnp.where(lane >= RW_HD, x, zero)], axis=0)


def _fold(z):
    lane = lax.broadcasted_iota(jnp.int32, (RW_HD, z.shape[1]), 1) % LANE
    return jnp.where(lane < RW_HD, z[:RW_HD], z[RW_HD:])


def _wkv_terms(r, lw, k, v, kkr, a, nb):
    n = CHUNK
    blk = nb * n
    rin = lax.broadcasted_iota(jnp.int32, (blk, LANE), 0) % n
    r2 = lax.broadcasted_iota(jnp.int32, (LANE, LANE), 0)
    c2 = lax.broadcasted_iota(jnp.int32, (LANE, LANE), 1)
    ones_bd = jnp.where((r2 // RW_HD) == (c2 // RW_HD), 1.0, 0.0).astype(BF16)
    rp = lax.broadcasted_iota(jnp.int32, (n, LANE), 0)
    cp = lax.broadcasted_iota(jnp.int32, (n, LANE), 1) % RW_HD
    strict = cp < rp
    incl = cp <= rp
    eye = cp == rp
    r4 = lax.broadcasted_iota(jnp.int32, (2 * n, LANE), 0)
    c4 = lax.broadcasted_iota(jnp.int32, (2 * n, LANE), 1) % RW_HD
    mask_ab = c4 < (r4 % n) + r4 // n

    kk = kkr * lax.rsqrt(jnp.maximum(_head_sum(kkr * kkr, ones_bd), 1e-24))
    b = kk * a
    cl = lw
    s = 1
    while s < n:
        cl = cl + jnp.where(rin >= s, pltpu.roll(cl, s, axis=0), 0.0)
        s *= 2
    ct = jnp.broadcast_to(cl.reshape(nb, n, LANE)[:, n - 1:n, :], (nb, n, LANE)).reshape(blk, LANE)
    kd = kk * jnp.exp(cl - lw)
    rd = r * jnp.exp(cl)
    einv = jnp.exp(-cl)
    ki = k * einv
    bi = b * einv
    e_end = jnp.exp(ct - cl)
    ki_e = k * e_end
    bi_e = b * e_end
    ect = jnp.exp(ct)

    rng = range(nb)
    sl = lambda x, c: x[c * n:(c + 1) * n]
    v_bd = [_bd(_bf(sl(v, c))) for c in rng]
    kr = [_bf(jnp.concatenate([sl(kd, c), sl(rd, c)], axis=0)) for c in rng]
    g1 = [jnp.where(mask_ab, _dot(kr[c], _bd(_bf(sl(bi, c))), NT), 0.0) for c in rng]
    g2 = [jnp.where(mask_ab, _dot(kr[c], _bd(_bf(sl(ki, c))), NT), 0.0) for c in rng]
    a_b = [g1[c][:n] for c in rng]

    blk_mask = lambda w: (rp // w) == (cp // w)
    nk = [jnp.where(blk_mask(INV_BASE), -a_b[c], 0.0) for c in rng]
    t = [jnp.where(eye, 1.0, 0.0) + nk[c] for c in rng]
    s = 2
    while s < INV_BASE:
        nk = [_dot(_bf(nk[c]), _bd(_bf(nk[c]))) for c in rng]
        t = [t[c] + _dot(_bf(t[c]), _bd(_bf(nk[c]))) for c in rng]
        s *= 2
    w = INV_BASE
    while w < n:
        off = blk_mask(2 * w) & jnp.logical_not(blk_mask(w))
        t_bf = [_bf(x) for x in t]
        lt = [_dot(_bf(jnp.where(off, a_b[c], 0.0)), _bd(t_bf[c])) for c in rng]
        t = [t[c] - _dot(t_bf[c], _bd(_bf(lt[c]))) for c in rng]
        w *= 2

    abkv = [_dot(_bf(g2[c]), v_bd[c]) for c in rng]
    x = [_bf(_dot(_bf(t[c]), jnp.concatenate([_bd(_bf(sl(kd, c))), _bd(_bf(abkv[c][:n]))], axis=1)))
         for c in rng]
    ry = [_dot(_bf(g1[c][n:]), jnp.concatenate([_bd(x[c][:, :LANE]), _bd(x[c][:, LANE:])], axis=1))
          for c in rng]
    bx = [_fold(_dot(_bf(sl(bi_e, c)), x[c], TN)) for c in rng]
    kv = [_fold(_dot(_bf(sl(ki_e, c)), _bf(sl(v, c)), TN)) for c in rng]
    m_c = [jnp.where(eye, ect[c * n:c * n + 1, :], 0.0) - bx[c][:, :LANE] for c in rng]
    n_c = [kv[c] - bx[c][:, LANE:] for c in rng]
    r_p = [sl(rd, c) - ry[c][:, :LANE] for c in rng]
    y_p = [abkv[c][n:] - ry[c][:, LANE:] for c in rng]
    rm = [_bf(jnp.concatenate([r_p[c], m_c[c]], axis=0)) for c in rng]
    return rm, n_c, y_p, ones_bd


def _wkv_epilogue(y, r, k, v, g, lnw, lnb, rk, ones_bd):
    inv_n = 1.0 / RW_HD
    mean = _head_sum(y, ones_bd) * inv_n
    yc = y - mean
    var = _head_sum(yc * yc, ones_bd) * inv_n
    yn = yc * lax.rsqrt(var + LNX_EPS) * lnw + lnb
    bonus = _head_sum(r * k * rk, ones_bd) * v
    return (yn + bonus) * g


def _rwkv_kernel(pr_ref, pk_ref, pv_ref, pl_ref, br_ref, bk_ref, bv_ref, bl_ref, mur_ref, muk_ref, muv_ref, mul_ref,
                 w0_ref, a0_ref, kk_ref, ka_ref, w2_ref, a2_ref, g2_ref, p0_ref, lnw_ref, lnb_ref, rk_ref,
                 y_ref, pt_ref, *scratch, nb, npair, carried, lora_tiles):
    if carried:
        p_scr, = scratch
        c_idx = pl.program_id(2)

        @pl.when(c_idx == 0)
        def _():
            for pp in range(npair):
                p_scr[pp] = p0_ref[0, pp]

    blk = nb * CHUNK
    run = blk if carried else CHUNK
    first = lax.broadcasted_iota(jnp.int32, (run, 1), 0) == 0

    def shifted(p_ref, b_ref, mu_ref):
        mu = mu_ref[...]
        outs = []
        for s in range(blk // run):
            sub = p_ref[run * s:run * (s + 1), :]
            prev = jnp.where(first, b_ref[s:s + 1, :], pltpu.roll(sub, 1, axis=0))
            outs.append(sub + (prev - sub) * mu)
        return outs[0] if len(outs) == 1 else jnp.concatenate(outs, axis=0)

    r = shifted(pr_ref, br_ref, mur_ref)
    xk = shifted(pk_ref, bk_ref, muk_ref)
    v = shifted(pv_ref, bv_ref, muv_ref)
    xl = shifted(pl_ref, bl_ref, mul_ref)
    (w_lo, w_hi), (a_lo, a_hi), (g_lo, g_hi) = lora_tiles
    dw = jnp.dot(jnp.tanh(xl[:, w_lo:w_hi]).astype(BF16), w2_ref[w_lo:w_hi, :], preferred_element_type=F32)
    da = jnp.dot(xl[:, a_lo:a_hi].astype(BF16), a2_ref[a_lo:a_hi, :], preferred_element_type=F32)
    g = jnp.dot(jax.nn.sigmoid(xl[:, g_lo:g_hi]).astype(BF16), g2_ref[g_lo:g_hi, :], preferred_element_type=F32)
    w_log = -jax.nn.softplus(-(w0_ref[...] + dw)) - 0.5
    lw = -jnp.exp(w_log)
    a = jax.nn.sigmoid(a0_ref[...] + da)
    k = xk * (1.0 + (a - 1.0) * ka_ref[...])
    kkr = xk * kk_ref[...]

    terms = []
    for pp in range(npair):
        ln = slice(pp * LANE, (pp + 1) * LANE)
        terms.append(_wkv_terms(r[:, ln], lw[:, ln], k[:, ln], v[:, ln], kkr[:, ln], a[:, ln], nb))
    p = [p_scr[pp] for pp in range(npair)] if carried else None
    ys = [[] for _ in range(npair)]
    for c in range(nb):
        for pp in range(npair):
            rm, n_c, y_p, _ = terms[pp]
            yp = _dot(rm[c], _bd(_bf(p[pp] if carried else p0_ref[c, pp])))
            ys[pp].append(yp[:CHUNK] + y_p[c])
            p_new = yp[CHUNK:] + n_c[c]
            if carried:
                p[pp] = p_new
            else:
                pt_ref[c, pp] = p_new
    for pp in range(npair):
        ln = slice(pp * LANE, (pp + 1) * LANE)
        y = jnp.concatenate(ys[pp], axis=0) if nb > 1 else ys[pp][0]
        y_ref[:, ln] = _wkv_epilogue(y, r[:, ln], k[:, ln], v[:, ln], g[:, ln], lnw_ref[:, ln],
                                     lnb_ref[:, ln], rk_ref[:, ln], terms[pp][3]).astype(y_ref.dtype)
    if carried:
        for pp in range(npair):
            p_scr[pp] = p[pp]

        @pl.when(c_idx == pl.num_programs(2) - 1)
        def _():
            for pp in range(npair):
                pt_ref[0, pp] = p[pp]


def rwkv_mix(p_rw, last_rows, shift0, mu, w0, a0, k_k, k_a, w2p, a2p, g2p, layer, pt0, lnw, lnb, rk, *,
             row0, n_seq, t_seq, rw_w, lora_w, lora_dims, nb_pref=16, npair_pref=4):
    carried = t_seq > CHUNK
    cols = p_rw.shape[1]
    n_pairs = rw_w // LANE
    npair = _tile(n_pairs, npair_pref, 1)
    gw = npair * LANE
    ncb = rw_w // gw
    lora_blk = (3 * rw_w) // lora_w
    if carried:
        blk = _tile(t_seq, nb_pref * CHUNK, CHUNK)
        nb, n_blk = blk // CHUNK, t_seq // blk
        grid = (n_seq, ncb, n_blk)
        step = lambda *g: g[0] * n_blk + g[2]
        ends = last_rows.reshape(n_seq, t_seq // CHUNK, cols)[:, nb - 1::nb]
        bnd = jnp.concatenate([shift0[:, None], ends[:, :-1]], axis=1).reshape(n_seq * n_blk, 1, cols)
        st = pl.BlockSpec((1, npair, RW_HD, LANE), lambda *g: (g[0], g[1], 0, 0))
        sem = ("parallel", "parallel", "arbitrary")
        scratch = [pltpu.VMEM((npair, RW_HD, LANE), F32)]
    else:
        nb = _tile(n_seq, nb_pref, 1)
        blk = nb * CHUNK
        grid = (n_seq // nb, ncb)
        step = lambda *g: g[0]
        bnd = shift0.reshape(n_seq // nb, nb, cols)
        st = pl.BlockSpec((nb, npair, RW_HD, LANE), lambda *g: (g[0], g[1], 0, 0))
        sem = ("parallel", "parallel")
        scratch = []
    base = row0 // blk
    nbnd = bnd.shape[1]
    tile = lambda lo, hi: (lo // LANE * LANE, -(-hi // LANE) * LANE)
    w_l, a_l, g_l = lora_dims
    lora_tiles = (tile(0, w_l), tile(w_l, w_l + a_l), tile(w_l + a_l, w_l + a_l + g_l))

    def tok(sec):
        return pl.BlockSpec((blk, gw), lambda *g: (base + step(*g), sec * ncb + g[1]))

    def bound(sec):
        return pl.BlockSpec((None, nbnd, gw), lambda *g: (step(*g), 0, sec * ncb + g[1]))

    def mus(sec):
        return pl.BlockSpec((1, gw), lambda *g: (0, sec * ncb + g[1]))

    vec = pl.BlockSpec((1, gw), lambda *g: (0, g[1]))
    wsp = _layer_spec((lora_w, gw), lambda *g: (0, g[1]), layer)
    in_specs = ([tok(0), tok(1), tok(2), pl.BlockSpec((blk, lora_w), lambda *g: (base + step(*g), lora_blk))]
                + [bound(0), bound(1), bound(2),
                   pl.BlockSpec((None, nbnd, lora_w), lambda *g: (step(*g), 0, lora_blk))]
                + [mus(0), mus(1), mus(2), pl.BlockSpec((1, lora_w), lambda *g: (0, lora_blk))]
                + [vec] * 4 + [wsp] * 3 + [st, vec, vec, vec])
    return pl.pallas_call(
        functools.partial(_rwkv_kernel, nb=nb, npair=npair, carried=carried, lora_tiles=lora_tiles),
        grid=grid,
        in_specs=in_specs,
        out_specs=[pl.BlockSpec((blk, gw), lambda *g: (step(*g), g[1])), st],
        out_shape=[jax.ShapeDtypeStruct((n_seq * t_seq, rw_w), BF16), jax.ShapeDtypeStruct(pt0.shape, F32)],
        scratch_shapes=scratch,
        compiler_params=_params(*sem),
        name="rwkv_seq" if carried else "rwkv_par",
    )(p_rw, p_rw, p_rw, p_rw, bnd, bnd, bnd, bnd, mu, mu, mu, mu, w0, a0, k_k, k_a, w2p, a2p, g2p,
      pt0, lnw, lnb, rk)


def _rope(x, cos2, sin2):
    half = QK_ROPE // 2
    rot = jnp.concatenate([-x[:, half:], x[:, :half]], axis=1)
    return x * cos2 + rot * sin2


def _q_prep_kernel(pq_ref, g_ref, w_ref, cos_ref, sin_ref, gn_ref, gr_ref, o_ref, *, n_heads):
    x = pq_ref[...]
    ms = jnp.mean(x * x, axis=-1, keepdims=True)
    xn = (x * lax.rsqrt(ms + NORM_EPS) * g_ref[...]).astype(BF16)
    cos2 = cos_ref[...]
    sin2 = sin_ref[...]
    gn = gn_ref[...]
    gr = gr_ref[...]
    for h in range(n_heads):
        q = jnp.dot(xn, w_ref[h], preferred_element_type=F32)
        nope = q[:, :QK_NOPE]
        rp = _rope(q[:, QK_NOPE:], cos2, sin2)
        ss = jnp.sum(nope * nope, axis=-1, keepdims=True) + jnp.sum(rp * rp, axis=-1, keepdims=True)
        scale = lax.rsqrt(ss * (1.0 / QK_DIM) + NORM_EPS) * QK_SCALE
        o_ref[h] = jnp.concatenate([nope * gn * scale, rp * gr * scale], axis=1).astype(o_ref.dtype)


def q_prep(p_q, g_qa, w_uq_h, cos2, sin2, gn, gr):
    m, ql = p_q.shape
    n_heads = w_uq_h.shape[0]
    tm = _tile(m, 256, 16)
    return pl.pallas_call(
        functools.partial(_q_prep_kernel, n_heads=n_heads),
        grid=(m // tm,),
        in_specs=[pl.BlockSpec((tm, ql), lambda i: (i, 0)),
                  pl.BlockSpec((1, ql), lambda i: (0, 0)),
                  pl.BlockSpec((n_heads, ql, QK_DIM), lambda i: (0, 0, 0)),
                  pl.BlockSpec((tm, QK_ROPE), lambda i: (i, 0)),
                  pl.BlockSpec((tm, QK_ROPE), lambda i: (i, 0)),
                  pl.BlockSpec((1, QK_NOPE), lambda i: (0, 0)),
                  pl.BlockSpec((1, QK_ROPE), lambda i: (0, 0))],
        out_specs=pl.BlockSpec((n_heads, tm, QK_DIM), lambda i: (0, i, 0)),
        out_shape=jax.ShapeDtypeStruct((n_heads, m, QK_DIM), BF16),
        compiler_params=_params("parallel"),
        name="q_prep",
    )(p_q, g_qa, w_uq_h, cos2, sin2, gn, gr)


def _kv_prep_kernel(p_ref, g_ref, cos_ref, sin_ref, ckv_ref, kr_ref, *, kv_lora):
    x = p_ref[:, :kv_lora]
    ms = jnp.mean(x * x, axis=-1, keepdims=True)
    ckv_ref[...] = x * lax.rsqrt(ms + NORM_EPS) * g_ref[...]
    kr_ref[...] = _rope(p_ref[:, kv_lora:kv_lora + QK_ROPE], cos_ref[...], sin_ref[...])


def kv_prep(p_kvr, g_kva, cos2, sin2, kv_lora):
    m, wd = p_kvr.shape
    tm = _tile(m, 512, 8)
    return pl.pallas_call(
        functools.partial(_kv_prep_kernel, kv_lora=kv_lora),
        grid=(m // tm,),
        in_specs=[pl.BlockSpec((tm, wd), lambda i: (i, 0)),
                  pl.BlockSpec((1, kv_lora), lambda i: (0, 0)),
                  pl.BlockSpec((tm, QK_ROPE), lambda i: (i, 0)),
                  pl.BlockSpec((tm, QK_ROPE), lambda i: (i, 0))],
        out_specs=[pl.BlockSpec((tm, kv_lora), lambda i: (i, 0)),
                   pl.BlockSpec((tm, QK_ROPE), lambda i: (i, 0))],
        out_shape=[jax.ShapeDtypeStruct((m, kv_lora), F32),
                   jax.ShapeDtypeStruct((m, QK_ROPE), F32)],
        compiler_params=_params("parallel"),
        name="kv_prep",
    )(p_kvr, g_kva, cos2, sin2)


def _kv_expand_kernel(c_ref, kr_ref, w_ref, gn_ref, gr_ref, k_ref, v_ref, *, hb):
    hw = QK_NOPE + V_HD
    kv = jnp.dot(c_ref[...].astype(BF16), w_ref[...], preferred_element_type=F32)
    kr = kr_ref[...]
    ss_r = jnp.sum(kr * kr, axis=-1, keepdims=True)
    krg = kr * gr_ref[...]
    for h in range(hb):
        kn = kv[:, h * hw:h * hw + QK_NOPE]
        rstd = lax.rsqrt((jnp.sum(kn * kn, axis=-1, keepdims=True) + ss_r) * (1.0 / QK_DIM) + NORM_EPS)
        k_ref[h] = jnp.concatenate([kn * gn_ref[...] * rstd, krg * rstd], axis=1).astype(k_ref.dtype)
        v_ref[h] = kv[:, h * hw + QK_NOPE:(h + 1) * hw].astype(v_ref.dtype)


def kv_expand(ckv, krope, w_ukv, layer, gn, gr, n_heads, rows=None):
    kv_lora = ckv.shape[1]
    rws = ckv.shape[0] if rows is None else rows
    tr = _tile(rws, 1024, 16)
    hb = _tile(n_heads, 4, 1)
    hw = QK_NOPE + V_HD
    return pl.pallas_call(
        functools.partial(_kv_expand_kernel, hb=hb),
        grid=(rws // tr, n_heads // hb),
        in_specs=[pl.BlockSpec((tr, kv_lora), lambda i, h: (i, 0)),
                  pl.BlockSpec((tr, QK_ROPE), lambda i, h: (i, 0)),
                  _layer_spec((kv_lora, hb * hw), lambda i, h: (0, h), layer),
                  pl.BlockSpec((1, QK_NOPE), lambda i, h: (0, 0)),
                  pl.BlockSpec((1, QK_ROPE), lambda i, h: (0, 0))],
        out_specs=[pl.BlockSpec((hb, tr, QK_DIM), lambda i, h: (h, i, 0)),
                   pl.BlockSpec((hb, tr, V_HD), lambda i, h: (h, i, 0))],
        out_shape=[jax.ShapeDtypeStruct((n_heads, rws, QK_DIM), BF16),
                   jax.ShapeDtypeStruct((n_heads, rws, V_HD), BF16)],
        compiler_params=_params("parallel", "arbitrary"),
        name="kv_expand",
    )(ckv, krope, w_ukv, gn, gr)


def _attn_prompt_kernel(q_ref, k_ref, v_ref, o_ref, *, blk):
    i = pl.program_id(2)
    q = q_ref[0]

    def step(kb, vb, carry, mask):
        m, l, acc = carry
        s = lax.dot_general(q, kb, NT, preferred_element_type=F32)
        if mask is not None:
            s = jnp.where(mask, s, -1e30)
        m_new = jnp.maximum(m, jnp.max(s, axis=-1, keepdims=True))
        p = jnp.exp(s - m_new)
        alpha = jnp.exp(m - m_new)
        l = alpha * l + jnp.sum(p, axis=-1, keepdims=True)
        acc = alpha * acc + jnp.dot(p.astype(BF16), vb, preferred_element_type=F32)
        return m_new, l, acc

    def body(j, carry):
        off = pl.multiple_of(j * blk, blk)
        return step(k_ref[0, pl.ds(off, blk), :], v_ref[0, pl.ds(off, blk), :], carry, None)

    init = (jnp.full((blk, 1), -1e30, F32), jnp.zeros((blk, 1), F32), jnp.zeros((blk, V_HD), F32))
    carry = lax.fori_loop(0, i, body, init)
    qc = lax.broadcasted_iota(jnp.int32, (blk, blk), 0) // CHUNK
    kc = lax.broadcasted_iota(jnp.int32, (blk, blk), 1) // CHUNK
    off = pl.multiple_of(i * blk, blk)
    m, l, acc = step(k_ref[0, pl.ds(off, blk), :], v_ref[0, pl.ds(off, blk), :], carry, kc <= qc)
    o_ref[...] = (acc / l).astype(o_ref.dtype)


def attn_prompt(q, k, v, *, n_seq, t_seq):
    n_heads = q.shape[0]
    blk = _tile(t_seq, 1024, CHUNK)
    nq = t_seq // blk
    return pl.pallas_call(
        functools.partial(_attn_prompt_kernel, blk=blk),
        grid=(n_seq, n_heads, nq),
        in_specs=[pl.BlockSpec((1, blk, QK_DIM), lambda b, h, i: (h, b * nq + i, 0)),
                  pl.BlockSpec((1, t_seq, QK_DIM), lambda b, h, i: (h, b, 0)),
                  pl.BlockSpec((1, t_seq, V_HD), lambda b, h, i: (h, b, 0))],
        out_specs=pl.BlockSpec((blk, V_HD), lambda b, h, i: (b * nq + i, h)),
        out_shape=jax.ShapeDtypeStruct((n_seq * t_seq, n_heads * V_HD), BF16),
        compiler_params=_params("parallel", "parallel", "arbitrary"),
        name="attn_prompt",
    )(q, k, v)


def _attn_sample_kernel(q_ref, k_ref, v_ref, o_ref, *, hb):
    for h in range(hb):
        s = lax.dot_general(q_ref[h], k_ref[h], NT, preferred_element_type=F32)
        m = jnp.max(s, axis=-1, keepdims=True)
        p = jnp.exp(s - m)
        l = jnp.sum(p, axis=-1, keepdims=True)
        acc = jnp.dot(p.astype(BF16), v_ref[h], preferred_element_type=F32)
        o_ref[:, h * V_HD:(h + 1) * V_HD] = (acc / l).astype(o_ref.dtype)


def attn_sample(q, k, v, *, row0, n_seq, t_q, t_kv):
    n_heads = q.shape[0]
    hb = _tile(n_heads, 4, 1)
    base = row0 // t_q
    return pl.pallas_call(
        functools.partial(_attn_sample_kernel, hb=hb),
        grid=(n_seq, n_heads // hb),
        in_specs=[pl.BlockSpec((hb, t_q, QK_DIM), lambda b, h: (h, base + b, 0)),
                  pl.BlockSpec((hb, t_kv, QK_DIM), lambda b, h: (h, b, 0)),
                  pl.BlockSpec((hb, t_kv, V_HD), lambda b, h: (h, b, 0))],
        out_specs=pl.BlockSpec((t_q, hb * V_HD), lambda b, h: (b, h)),
        out_shape=jax.ShapeDtypeStruct((n_seq * t_q, n_heads * V_HD), BF16),
        compiler_params=_params("parallel", "parallel"),
        name="attn_sample",
    )(q, k, v)


def _rope_tables(positions):
    half = QK_ROPE // 2
    inv_freq = ROPE_THETA ** (-jnp.arange(half, dtype=F32) / half)
    ang = positions.astype(F32)[:, None] * inv_freq[None, :]
    cos, sin = jnp.cos(ang), jnp.sin(ang)
    return jnp.concatenate([cos, cos], axis=1), jnp.concatenate([sin, sin], axis=1)


def _pad_cols(w, n):
    return jnp.pad(w, [(0, 0)] * (w.ndim - 1) + [(0, n - w.shape[-1])])


def kernel(x_prompt, x_sample, cache_ckv, cache_krope, state_wkv, state_shift, norm_attn_g, w_in, rw_mu, rw_w0, rw_w2, rw_a0, rw_a2, rw_g2, rw_k_k, rw_k_a, rw_r_k, rw_lnx_w, rw_lnx_b, mla_q_a_g, mla_w_uq, mla_kv_a_g, mla_w_ukv, mla_q_norm_g, mla_k_norm_g, w_out, norm_ffn_g, w_gate, w_up, w_down):
    bp, tp, d = x_prompt.shape
    bs, ts, _ = x_sample.shape
    depth = w_in.shape[0]
    past = cache_ckv.shape[2]
    kv_lora = cache_ckv.shape[3]
    rw_h = rw_r_k.shape[1]
    rw_w = rw_h * RW_HD
    w_lora, a_lora, g_lora = rw_w2.shape[1], rw_a2.shape[1], rw_g2.shape[1]
    lora = w_lora + a_lora + g_lora
    lora_w = -(-lora // LANE) * LANE
    rw_cols = 3 * rw_w + lora
    rw_pad = 3 * rw_w + lora_w
    q_lora = mla_w_uq.shape[1]
    mla_h = mla_w_uq.shape[2] // QK_DIM
    assert (3 * rw_w) % lora_w == 0 and tp % CHUNK == 0 and ts == CHUNK
    mp, ms = bp * tp, bs * ts
    m = mp + ms
    kvr_w = -(-(kv_lora + QK_ROPE) // LANE) * LANE

    layers = range(depth)
    w_rw = _pad_cols(w_in[:, :, :rw_cols], rw_pad).astype(BF16)
    w_q = [w_in[l, :, rw_cols:rw_cols + q_lora].astype(BF16) for l in layers]
    w_kvr = [_pad_cols(w_in[l, :, rw_cols + q_lora:], kvr_w).astype(BF16) for l in layers]
    mu_p = _pad_cols(rw_mu, rw_pad).reshape(depth, 1, rw_pad)
    zl = lambda w, r0: jnp.zeros((depth, lora_w, rw_w), F32).at[:, r0:r0 + w.shape[1]].set(w).astype(BF16)
    w2p, a2p, g2p = zl(rw_w2, 0), zl(rw_a2, w_lora), zl(rw_g2, w_lora + a_lora)
    vec = lambda t: t.reshape(depth, 1, -1)
    w_uq_h = [mla_w_uq[l].reshape(q_lora, mla_h, QK_DIM).transpose(1, 0, 2).astype(BF16) for l in layers]
    w_ukv, w_o = mla_w_ukv.astype(BF16), w_out.astype(BF16)
    w_g, w_u, w_d = w_gate.astype(BF16), w_up.astype(BF16), w_down.astype(BF16)
    qn_n, qn_r = mla_q_norm_g[:, None, :QK_NOPE], jnp.tile(mla_q_norm_g[:, None, QK_NOPE:], (1, 1, 2))
    kn_n, kn_r = mla_k_norm_g[:, None, :QK_NOPE], jnp.tile(mla_k_norm_g[:, None, QK_NOPE:], (1, 1, 2))

    pos = jnp.concatenate([jnp.tile(jnp.arange(tp), bp), jnp.tile(past + jnp.arange(ts), bs)])
    cos2, sin2 = _rope_tables(pos)

    x = jnp.concatenate([x_prompt.reshape(mp, d), x_sample.reshape(ms, d)], axis=0)
    zero_shift = jnp.zeros((bp, rw_pad), F32)
    to_pairs = lambda s: (s.reshape(-1, rw_h // 2, 2, RW_HD, RW_HD).transpose(0, 1, 4, 2, 3)
                          .reshape(-1, rw_h // 2, RW_HD, 2 * RW_HD))
    from_pairs = lambda p: (p.reshape(-1, rw_h // 2, RW_HD, 2, RW_HD).transpose(0, 1, 3, 4, 2)
                            .reshape(-1, rw_h, RW_HD, RW_HD))
    zero_state = jnp.zeros((bp, rw_h // 2, RW_HD, 2 * RW_HD), F32)
    outs = [[] for _ in range(8)]

    for l in range(depth):
        h = rmsnorm_bf16(x, norm_attn_g[l])
        p_rw, last = matmul_with_block_ends(h, w_rw, l)
        p_q = matmul(h, w_q[l], tn_pref=q_lora)
        p_kvr = matmul(h, w_kvr[l], tn_pref=kvr_w)

        lnw, lnb, rk = vec(rw_lnx_w)[l], vec(rw_lnx_b)[l], rw_r_k[l].reshape(1, rw_w)
        rw_args = (mu_p[l], vec(rw_w0)[l], vec(rw_a0)[l], vec(rw_k_k)[l], vec(rw_k_a)[l], w2p, a2p, g2p, l)
        rw_dims = dict(rw_w=rw_w, lora_w=lora_w, lora_dims=(w_lora, a_lora, g_lora))
        y_rw_p, pst_p = rwkv_mix(p_rw, last[:mp // CHUNK], zero_shift, *rw_args, zero_state, lnw, lnb, rk,
                                 row0=0, n_seq=bp, t_seq=tp, **rw_dims)
        y_rw_s, pst_s = rwkv_mix(p_rw, last[mp // CHUNK:], _pad_cols(state_shift[l], rw_pad), *rw_args,
                                 to_pairs(state_wkv[l]), lnw, lnb, rk, row0=mp, n_seq=bs, t_seq=ts, **rw_dims)

        q = q_prep(p_q, vec(mla_q_a_g)[l], w_uq_h[l], cos2, sin2, qn_n[l], qn_r[l])
        ckv_new, kr_new = kv_prep(p_kvr, vec(mla_kv_a_g)[l], cos2, sin2, kv_lora)
        k_p, v_p = kv_expand(ckv_new, kr_new, w_ukv, l, kn_n[l], kn_r[l], mla_h, rows=mp)
        y_mla_p = attn_prompt(q, k_p, v_p, n_seq=bp, t_seq=tp)
        ckv_s = jnp.concatenate([cache_ckv[l], ckv_new[mp:].reshape(bs, ts, kv_lora)], axis=1)
        kr_s = jnp.concatenate([cache_krope[l], kr_new[mp:].reshape(bs, ts, QK_ROPE)], axis=1)
        t_kv = past + ts
        k_s, v_s = kv_expand(ckv_s.reshape(bs * t_kv, kv_lora), kr_s.reshape(bs * t_kv, QK_ROPE),
                             w_ukv, l, kn_n[l], kn_r[l], mla_h)
        y_mla_s = attn_sample(q, k_s, v_s, row0=mp, n_seq=bs, t_q=ts, t_kv=t_kv)

        x = out_proj(y_rw_p, y_rw_s, y_mla_p, y_mla_s, w_o, l, x)

        h2 = rmsnorm_bf16(x, norm_ffn_g[l])
        u = ffn_up(h2, w_g, w_u, l)
        x = matmul_res_ksplit(u, w_d, l, x)

        outs[0].append(ckv_new[:mp].reshape(bp, tp, kv_lora))
        outs[1].append(kr_new[:mp].reshape(bp, tp, QK_ROPE))
        outs[2].append(from_pairs(pst_p))
        outs[3].append(p_rw[tp - 1:mp:tp, :rw_cols])
        outs[4].append(ckv_new[mp:].reshape(bs, ts, kv_lora))
        outs[5].append(kr_new[mp:].reshape(bs, ts, QK_ROPE))
        outs[6].append(from_pairs(pst_s))
        outs[7].append(p_rw[mp + ts - 1::ts, :rw_cols])

    return (x[:mp].reshape(bp, tp, d), x[mp:].reshape(bs, ts, d)) + tuple(jnp.stack(o) for o in outs)
```

```python
import functools
import math

import jax
import jax.numpy as jnp
from jax import lax
from jax.experimental import pallas as pl
from jax.experimental.pallas import tpu as pltpu

F32 = jnp.float32
BF16 = jnp.bfloat16

CHUNK = 64
RW_HD = 64
QK_NOPE = 128
QK_ROPE = 64
QK_DIM = QK_NOPE + QK_ROPE
V_HD = 128
ROPE_THETA = 10000.0
NORM_EPS = 1e-6
LNX_EPS = 64e-5
QK_SCALE = QK_DIM ** -0.5

LANE = 128
VMEM_LIMIT = 56 * 1024 * 1024

NN = (((1,), (0,)), ((), ()))
NT = (((1,), (1,)), ((), ()))
TN = (((0,), (0,)), ((), ()))


def _tile(n, pref, quantum):
    if n <= pref:
        return n
    t = (pref // quantum) * quantum
    while t >= quantum:
        if n % t == 0:
            return t
        t -= quantum
    raise ValueError(f"no tile for {n} with quantum {quantum}")


def _layer_spec(block, index, layer):
    return pl.BlockSpec((None,) + block, lambda *g: (layer,) + index(*g))


def _params(*sem):
    return pltpu.CompilerParams(dimension_semantics=sem, vmem_limit_bytes=VMEM_LIMIT)


def _rmsnorm_kernel(x_ref, g_ref, o_ref):
    x = x_ref[...]
    ms = jnp.mean(x * x, axis=-1, keepdims=True)
    o_ref[...] = (x * lax.rsqrt(ms + NORM_EPS) * g_ref[...]).astype(o_ref.dtype)


def rmsnorm_bf16(x, g):
    m, d = x.shape
    tm = _tile(m, 256, 8)
    return pl.pallas_call(
        _rmsnorm_kernel,
        grid=(m // tm,),
        in_specs=[pl.BlockSpec((tm, d), lambda i: (i, 0)),
                  pl.BlockSpec((1, d), lambda i: (0, 0))],
        out_specs=pl.BlockSpec((tm, d), lambda i: (i, 0)),
        out_shape=jax.ShapeDtypeStruct((m, d), BF16),
        compiler_params=_params("parallel"),
        name="rmsnorm",
    )(x, g.reshape(1, d))


def _mm_kernel(x_ref, w_ref, o_ref):
    o_ref[...] = jnp.dot(x_ref[...], w_ref[...], preferred_element_type=F32).astype(o_ref.dtype)


def _mm_res_kernel(x_ref, w_ref, r_ref, o_ref):
    o_ref[...] = r_ref[...] + jnp.dot(x_ref[...], w_ref[...], preferred_element_type=F32)


def _out_proj_kernel(y1p_ref, y1s_ref, y2p_ref, y2s_ref, w1_ref, w2_ref, r_ref, o_ref, *, n_p):
    def go(y1_ref, y2_ref):
        acc = jnp.dot(y1_ref[...], w1_ref[...], preferred_element_type=F32)
        acc = acc + jnp.dot(y2_ref[...], w2_ref[...], preferred_element_type=F32)
        o_ref[...] = r_ref[...] + acc

    @pl.when(pl.program_id(0) < n_p)
    def _():
        go(y1p_ref, y2p_ref)

    @pl.when(pl.program_id(0) >= n_p)
    def _():
        go(y1s_ref, y2s_ref)


def _ffn_up_kernel(h_ref, wg_ref, wu_ref, o_ref):
    h = h_ref[...]
    a = jnp.dot(h, wg_ref[...], preferred_element_type=F32)
    b = jnp.dot(h, wu_ref[...], preferred_element_type=F32)
    o_ref[...] = (a * jax.nn.sigmoid(a) * b).astype(o_ref.dtype)


def matmul(x, w, *, tm_pref=1024, tn_pref=512, out_dtype=F32, res=None):
    m, k = x.shape
    n = w.shape[1]
    tm = _tile(m, tm_pref, 8)
    tn = _tile(n, tn_pref, LANE)
    in_specs = [pl.BlockSpec((tm, k), lambda i, j: (i, 0)),
                pl.BlockSpec((k, tn), lambda i, j: (0, j))]
    args = [x, w]
    body = _mm_kernel
    if res is not None:
        in_specs.append(pl.BlockSpec((tm, tn), lambda i, j: (i, j)))
        args.append(res)
        body = _mm_res_kernel
    return pl.pallas_call(
        body,
        grid=(m // tm, n // tn),
        in_specs=in_specs,
        out_specs=pl.BlockSpec((tm, tn), lambda i, j: (i, j)),
        out_shape=jax.ShapeDtypeStruct((m, n), out_dtype),
        compiler_params=_params("parallel", "arbitrary"),
        name="matmul",
    )(*args)


def _mm_res_ksplit_kernel(x_ref, w_ref, r_ref, o_ref):
    part = jnp.dot(x_ref[...], w_ref[...], preferred_element_type=F32)

    @pl.when(pl.program_id(2) == 0)
    def _():
        o_ref[...] = r_ref[...] + part

    @pl.when(pl.program_id(2) > 0)
    def _():
        o_ref[...] += part


def matmul_res_ksplit(x, w, layer, res, *, n_k=2, tm_pref=1024, tn_pref=512):
    m, k = x.shape
    n = w.shape[2]
    tm = _tile(m, tm_pref, 8)
    tn = _tile(n, tn_pref, LANE)
    assert k % (n_k * LANE) == 0
    tk = k // n_k
    return pl.pallas_call(
        _mm_res_ksplit_kernel,
        grid=(m // tm, n // tn, n_k),
        in_specs=[pl.BlockSpec((tm, tk), lambda i, j, kk: (i, kk)),
                  _layer_spec((tk, tn), lambda i, j, kk: (kk, j), layer),
                  pl.BlockSpec((tm, tn), lambda i, j, kk: (i, j))],
        out_specs=pl.BlockSpec((tm, tn), lambda i, j, kk: (i, j)),
        out_shape=jax.ShapeDtypeStruct((m, n), F32),
        compiler_params=_params("parallel", "arbitrary", "arbitrary"),
        name="matmul_ksplit",
    )(x, w, res)


def _mm_last_kernel(x_ref, w_ref, o_ref, b_ref):
    o_ref[...] = jnp.dot(x_ref[...], w_ref[...], preferred_element_type=F32)
    for s in range(b_ref.shape[0]):
        b_ref[s:s + 1, :] = o_ref[CHUNK * s + CHUNK - 1:CHUNK * (s + 1), :]


def matmul_with_block_ends(x, w, layer):
    m, k = x.shape
    n = w.shape[2]
    tm = _tile(m, 1024, 8 * CHUNK)
    tn = _tile(n, 512, LANE)
    return pl.pallas_call(
        _mm_last_kernel,
        grid=(m // tm, n // tn),
        in_specs=[pl.BlockSpec((tm, k), lambda i, j: (i, 0)),
                  _layer_spec((k, tn), lambda i, j: (0, j), layer)],
        out_specs=[pl.BlockSpec((tm, tn), lambda i, j: (i, j)),
                   pl.BlockSpec((tm // CHUNK, tn), lambda i, j: (i, j))],
        out_shape=[jax.ShapeDtypeStruct((m, n), F32), jax.ShapeDtypeStruct((m // CHUNK, n), F32)],
        compiler_params=_params("parallel", "arbitrary"),
        name="matmul_ends",
    )(x, w)


def out_proj(y1p, y1s, y2p, y2s, w, layer, res):
    mp, k1 = y1p.shape
    ms, k2 = y2s.shape
    n = w.shape[2]
    assert k1 == k2 and w.shape[1] == k1 + k2
    tm = _tile(math.gcd(mp, ms), 1024, 8)
    tn = _tile(n, 512, LANE)
    n_p = mp // tm
    row_p = lambda i, j: (jnp.minimum(i, n_p - 1), 0)
    row_s = lambda i, j: (jnp.maximum(i - n_p, 0), 0)
    return pl.pallas_call(
        functools.partial(_out_proj_kernel, n_p=n_p),
        grid=((mp + ms) // tm, n // tn),
        in_specs=[pl.BlockSpec((tm, k1), row_p), pl.BlockSpec((tm, k1), row_s),
                  pl.BlockSpec((tm, k2), row_p), pl.BlockSpec((tm, k2), row_s),
                  _layer_spec((k1, tn), lambda i, j: (0, j), layer),
                  _layer_spec((k2, tn), lambda i, j: (1, j), layer),
                  pl.BlockSpec((tm, tn), lambda i, j: (i, j))],
        out_specs=pl.BlockSpec((tm, tn), lambda i, j: (i, j)),
        out_shape=jax.ShapeDtypeStruct((mp + ms, n), F32),
        compiler_params=_params("parallel", "arbitrary"),
        name="out_proj",
    )(y1p, y1s, y2p, y2s, w, w, res)


def ffn_up(h, wg, wu, layer):
    m, k = h.shape
    n = wg.shape[2]
    tm = _tile(m, 2304, 16)
    tn = _tile(n, 256, LANE)
    return pl.pallas_call(
        _ffn_up_kernel,
        grid=(m // tm, n // tn),
        in_specs=[pl.BlockSpec((tm, k), lambda i, j: (i, 0)),
                  _layer_spec((k, tn), lambda i, j: (0, j), layer),
                  _layer_spec((k, tn), lambda i, j: (0, j), layer)],
        out_specs=pl.BlockSpec((tm, tn), lambda i, j: (i, j)),
        out_shape=jax.ShapeDtypeStruct((m, n), BF16),
        compiler_params=_params("parallel", "arbitrary"),
        name="ffn_up",
    )(h, wg, wu)


INV_BASE = 8


def _bf(x):
    return x.astype(BF16)


def _dot(a, b, dims=NN):
    return lax.dot_general(a, b, dims, preferred_element_type=F32)


def _head_sum(x, ones_bd):
    hi = _bf(x)
    lo = _bf(x - hi.astype(F32))
    return _dot(hi, ones_bd) + _dot(lo, ones_bd)


def _bd(x):
    lane = lax.broadcasted_iota(jnp.int32, x.shape, 1)
    zero = jnp.zeros_like(x)
    return jnp.concatenate([jnp.where(lane < RW_HD, x, zero), jnp.where(lane >= RW_HD, x, zero)], axis=0)


def _fold(z):
    lane = lax.broadcasted_iota(jnp.int32, (RW_HD, z.shape[1]), 1) % LANE
    return jnp.where(lane < RW_HD, z[:RW_HD], z[RW_HD:])


def _wkv_terms(r, lw, k, v, kkr, a, nb):
    n = CHUNK
    blk = nb * n
    rin = lax.broadcasted_iota(jnp.int32, (blk, LANE), 0) % n
    r2 = lax.broadcasted_iota(jnp.int32, (LANE, LANE), 0)
    c2 = lax.broadcasted_iota(jnp.int32, (LANE, LANE), 1)
    ones_bd = jnp.where((r2 // RW_HD) == (c2 // RW_HD), 1.0, 0.0).astype(BF16)
    rp = lax.broadcasted_iota(jnp.int32, (n, LANE), 0)
    cp = lax.broadcasted_iota(jnp.int32, (n, LANE), 1) % RW_HD
    strict = cp < rp
    incl = cp <= rp
    eye = cp == rp
    r4 = lax.broadcasted_iota(jnp.int32, (2 * n, LANE), 0)
    c4 = lax.broadcasted_iota(jnp.int32, (2 * n, LANE), 1) % RW_HD
    mask_ab = c4 < (r4 % n) + r4 // n

    kk = kkr * lax.rsqrt(jnp.maximum(_head_sum(kkr * kkr, ones_bd), 1e-24))
    b = kk * a
    cl = lw
    s = 1
    while s < n:
        cl = cl + jnp.where(rin >= s, pltpu.roll(cl, s, axis=0), 0.0)
        s *= 2
    ct = jnp.broadcast_to(cl.reshape(nb, n, LANE)[:, n - 1:n, :], (nb, n, LANE)).reshape(blk, LANE)
    kd = kk * jnp.exp(cl - lw)
    rd = r * jnp.exp(cl)
    einv = jnp.exp(-cl)
    ki = k * einv
    bi = b * einv
    e_end = jnp.exp(ct - cl)
    ki_e = k * e_end
    bi_e = b * e_end
    ect = jnp.exp(ct)

    rng = range(nb)
    sl = lambda x, c: x[c * n:(c + 1) * n]
    v_bd = [_bd(_bf(sl(v, c))) for c in rng]
    kr = [_bf(jnp.concatenate([sl(kd, c), sl(rd, c)], axis=0)) for c in rng]
    g1 = [jnp.where(mask_ab, _dot(kr[c], _bd(_bf(sl(bi, c))), NT), 0.0) for c in rng]
    g2 = [jnp.where(mask_ab, _dot(kr[c], _bd(_bf(sl(ki, c))), NT), 0.0) for c in rng]
    a_b = [g1[c][:n] for c in rng]

    blk_mask = lambda w: (rp // w) == (cp // w)
    nk = [jnp.where(blk_mask(INV_BASE), -a_b[c], 0.0) for c in rng]
    t = [jnp.where(eye, 1.0, 0.0) + nk[c] for c in rng]
    s = 2
    while s < INV_BASE:
        nk = [_dot(_bf(nk[c]), _bd(_bf(nk[c]))) for c in rng]
        t = [t[c] + _dot(_bf(t[c]), _bd(_bf(nk[c]))) for c in rng]
        s *= 2
    w = INV_BASE
    while w < n:
        off = blk_mask(2 * w) & jnp.logical_not(blk_mask(w))
        t_bf = [_bf(x) for x in t]
        lt = [_dot(_bf(jnp.where(off, a_b[c], 0.0)), _bd(t_bf[c])) for c in rng]
        t = [t[c] - _dot(t_bf[c], _bd(_bf(lt[c]))) for c in rng]
        w *= 2

    abkv = [_dot(_bf(g2[c]), v_bd[c]) for c in rng]
    x = [_bf(_dot(_bf(t[c]), jnp.concatenate([_bd(_bf(sl(kd, c))), _bd(_bf(abkv[c][:n]))], axis=1)))
         for c in rng]
    ry = [_dot(_bf(g1[c][n:]), jnp.concatenate([_bd(x[c][:, :LANE]), _bd(x[c][:, LANE:])], axis=1))
          for c in rng]
    bx = [_fold(_dot(_bf(sl(bi_e, c)), x[c], TN)) for c in rng]
    kv = [_fold(_dot(_bf(sl(ki_e, c)), _bf(sl(v, c)), TN)) for c in rng]
    m_c = [jnp.where(eye, ect[c * n:c * n + 1, :], 0.0) - bx[c][:, :LANE] for c in rng]
    n_c = [kv[c] - bx[c][:, LANE:] for c in rng]
    r_p = [sl(rd, c) - ry[c][:, :LANE] for c in rng]
    y_p = [abkv[c][n:] - ry[c][:, LANE:] for c in rng]
    rm = [_bf(jnp.concatenate([r_p[c], m_c[c]], axis=0)) for c in rng]
    return rm, n_c, y_p, ones_bd


def _wkv_epilogue(y, r, k, v, g, lnw, lnb, rk, ones_bd):
    inv_n = 1.0 / RW_HD
    mean = _head_sum(y, ones_bd) * inv_n
    yc = y - mean
    var = _head_sum(yc * yc, ones_bd) * inv_n
    yn = yc * lax.rsqrt(var + LNX_EPS) * lnw + lnb
    bonus = _head_sum(r * k * rk, ones_bd) * v
    return (yn + bonus) * g


def _rwkv_kernel(pr_ref, pk_ref, pv_ref, pl_ref, br_ref, bk_ref, bv_ref, bl_ref, mur_ref, muk_ref, muv_ref, mul_ref,
                 w0_ref, a0_ref, kk_ref, ka_ref, w2_ref, a2_ref, g2_ref, p0_ref, lnw_ref, lnb_ref, rk_ref,
                 y_ref, pt_ref, *scratch, nb, npair, carried, lora_tiles):
    if carried:
        p_scr, = scratch
        c_idx = pl.program_id(2)

        @pl.when(c_idx == 0)
        def _():
            for pp in range(npair):
                p_scr[pp] = p0_ref[0, pp]

    blk = nb * CHUNK
    run = blk if carried else CHUNK
    first = lax.broadcasted_iota(jnp.int32, (run, 1), 0) == 0

    def shifted(p_ref, b_ref, mu_ref):
        mu = mu_ref[...]
        outs = []
        for s in range(blk // run):
            sub = p_ref[run * s:run * (s + 1), :]
            prev = jnp.where(first, b_ref[s:s + 1, :], pltpu.roll(sub, 1, axis=0))
            outs.append(sub + (prev - sub) * mu)
        return outs[0] if len(outs) == 1 else jnp.concatenate(outs, axis=0)

    r = shifted(pr_ref, br_ref, mur_ref)
    xk = shifted(pk_ref, bk_ref, muk_ref)
    v = shifted(pv_ref, bv_ref, muv_ref)
    xl = shifted(pl_ref, bl_ref, mul_ref)
    (w_lo, w_hi), (a_lo, a_hi), (g_lo, g_hi) = lora_tiles
    dw = jnp.dot(jnp.tanh(xl[:, w_lo:w_hi]).astype(BF16), w2_ref[w_lo:w_hi, :], preferred_element_type=F32)
    da = jnp.dot(xl[:, a_lo:a_hi].astype(BF16), a2_ref[a_lo:a_hi, :], preferred_element_type=F32)
    g = jnp.dot(jax.nn.sigmoid(xl[:, g_lo:g_hi]).astype(BF16), g2_ref[g_lo:g_hi, :], preferred_element_type=F32)
    w_log = -jax.nn.softplus(-(w0_ref[...] + dw)) - 0.5
    lw = -jnp.exp(w_log)
    a = jax.nn.sigmoid(a0_ref[...] + da)
    k = xk * (1.0 + (a - 1.0) * ka_ref[...])
    kkr = xk * kk_ref[...]

    terms = []
    for pp in range(npair):
        ln = slice(pp * LANE, (pp + 1) * LANE)
        terms.append(_wkv_terms(r[:, ln], lw[:, ln], k[:, ln], v[:, ln], kkr[:, ln], a[:, ln], nb))
    p = [p_scr[pp] for pp in range(npair)] if carried else None
    ys = [[] for _ in range(npair)]
    for c in range(nb):
        for pp in range(npair):
            rm, n_c, y_p, _ = terms[pp]
            yp = _dot(rm[c], _bd(_bf(p[pp] if carried else p0_ref[c, pp])))
            ys[pp].append(yp[:CHUNK] + y_p[c])
            p_new = yp[CHUNK:] + n_c[c]
            if carried:
                p[pp] = p_new
            else:
                pt_ref[c, pp] = p_new
    for pp in range(npair):
        ln = slice(pp * LANE, (pp + 1) * LANE)
        y = jnp.concatenate(ys[pp], axis=0) if nb > 1 else ys[pp][0]
        y_ref[:, ln] = _wkv_epilogue(y, r[:, ln], k[:, ln], v[:, ln], g[:, ln], lnw_ref[:, ln],
                                     lnb_ref[:, ln], rk_ref[:, ln], terms[pp][3]).astype(y_ref.dtype)
    if carried:
        for pp in range(npair):
            p_scr[pp] = p[pp]

        @pl.when(c_idx == pl.num_programs(2) - 1)
        def _():
            for pp in range(npair):
                pt_ref[0, pp] = p[pp]


def rwkv_mix(p_rw, last_rows, shift0, mu, w0, a0, k_k, k_a, w2p, a2p, g2p, layer, pt0, lnw, lnb, rk, *,
             row0, n_seq, t_seq, rw_w, lora_w, lora_dims, nb_pref=16, npair_pref=4):
    carried = t_seq > CHUNK
    cols = p_rw.shape[1]
    n_pairs = rw_w // LANE
    npair = _tile(n_pairs, npair_pref, 1)
    gw = npair * LANE
    ncb = rw_w // gw
    lora_blk = (3 * rw_w) // lora_w
    if carried:
        blk = _tile(t_seq, nb_pref * CHUNK, CHUNK)
        nb, n_blk = blk // CHUNK, t_seq // blk
        grid = (n_seq, ncb, n_blk)
        step = lambda *g: g[0] * n_blk + g[2]
        ends = last_rows.reshape(n_seq, t_seq // CHUNK, cols)[:, nb - 1::nb]
        bnd = jnp.concatenate([shift0[:, None], ends[:, :-1]], axis=1).reshape(n_seq * n_blk, 1, cols)
        st = pl.BlockSpec((1, npair, RW_HD, LANE), lambda *g: (g[0], g[1], 0, 0))
        sem = ("parallel", "parallel", "arbitrary")
        scratch = [pltpu.VMEM((npair, RW_HD, LANE), F32)]
    else:
        nb = _tile(n_seq, nb_pref, 1)
        blk = nb * CHUNK
        grid = (n_seq // nb, ncb)
        step = lambda *g: g[0]
        bnd = shift0.reshape(n_seq // nb, nb, cols)
        st = pl.BlockSpec((nb, npair, RW_HD, LANE), lambda *g: (g[0], g[1], 0, 0))
        sem = ("parallel", "parallel")
        scratch = []
    base = row0 // blk
    nbnd = bnd.shape[1]
    tile = lambda lo, hi: (lo // LANE * LANE, -(-hi // LANE) * LANE)
    w_l, a_l, g_l = lora_dims
    lora_tiles = (tile(0, w_l), tile(w_l, w_l + a_l), tile(w_l + a_l, w_l + a_l + g_l))

    def tok(sec):
        return pl.BlockSpec((blk, gw), lambda *g: (base + step(*g), sec * ncb + g[1]))

    def bound(sec):
        return pl.BlockSpec((None, nbnd, gw), lambda *g: (step(*g), 0, sec * ncb + g[1]))

    def mus(sec):
        return pl.BlockSpec((1, gw), lambda *g: (0, sec * ncb + g[1]))

    vec = pl.BlockSpec((1, gw), lambda *g: (0, g[1]))
    wsp = _layer_spec((lora_w, gw), lambda *g: (0, g[1]), layer)
    in_specs = ([tok(0), tok(1), tok(2), pl.BlockSpec((blk, lora_w), lambda *g: (base + step(*g), lora_blk))]
                + [bound(0), bound(1), bound(2),
                   pl.BlockSpec((None, nbnd, lora_w), lambda *g: (step(*g), 0, lora_blk))]
                + [mus(0), mus(1), mus(2), pl.BlockSpec((1, lora_w), lambda *g: (0, lora_blk))]
                + [vec] * 4 + [wsp] * 3 + [st, vec, vec, vec])
    return pl.pallas_call(
        functools.partial(_rwkv_kernel, nb=nb, npair=npair, carried=carried, lora_tiles=lora_tiles),
        grid=grid,
        in_specs=in_specs,
        out_specs=[pl.BlockSpec((blk, gw), lambda *g: (step(*g), g[1])), st],
        out_shape=[jax.ShapeDtypeStruct((n_seq * t_seq, rw_w), BF16), jax.ShapeDtypeStruct(pt0.shape, F32)],
        scratch_shapes=scratch,
        compiler_params=_params(*sem),
        name="rwkv_seq" if carried else "rwkv_par",
    )(p_rw, p_rw, p_rw, p_rw, bnd, bnd, bnd, bnd, mu, mu, mu, mu, w0, a0, k_k, k_a, w2p, a2p, g2p,
      pt0, lnw, lnb, rk)


def _rope(x, cos2, sin2):
    half = QK_ROPE // 2
    rot = jnp.concatenate([-x[:, half:], x[:, :half]], axis=1)
    return x * cos2 + rot * sin2


def _q_prep_kernel(pq_ref, g_ref, w_ref, cos_ref, sin_ref, gn_ref, gr_ref, o_ref, *, n_heads):
    x = pq_ref[...]
    ms = jnp.mean(x * x, axis=-1, keepdims=True)
    xn = (x * lax.rsqrt(ms + NORM_EPS) * g_ref[...]).astype(BF16)
    cos2 = cos_ref[...]
    sin2 = sin_ref[...]
    gn = gn_ref[...]
    gr = gr_ref[...]
    for h in range(n_heads):
        q = jnp.dot(xn, w_ref[h], preferred_element_type=F32)
        nope = q[:, :QK_NOPE]
        rp = _rope(q[:, QK_NOPE:], cos2, sin2)
        ss = jnp.sum(nope * nope, axis=-1, keepdims=True) + jnp.sum(rp * rp, axis=-1, keepdims=True)
        scale = lax.rsqrt(ss * (1.0 / QK_DIM) + NORM_EPS) * QK_SCALE
        o_ref[h] = jnp.concatenate([nope * gn * scale, rp * gr * scale], axis=1).astype(o_ref.dtype)


def q_prep(p_q, g_qa, w_uq_h, cos2, sin2, gn, gr):
    m, ql = p_q.shape
    n_heads = w_uq_h.shape[0]
    tm = _tile(m, 256, 16)
    return pl.pallas_call(
        functools.partial(_q_prep_kernel, n_heads=n_heads),
        grid=(m // tm,),
        in_specs=[pl.BlockSpec((tm, ql), lambda i: (i, 0)),
                  pl.BlockSpec((1, ql), lambda i: (0, 0)),
                  pl.BlockSpec((n_heads, ql, QK_DIM), lambda i: (0, 0, 0)),
                  pl.BlockSpec((tm, QK_ROPE), lambda i: (i, 0)),
                  pl.BlockSpec((tm, QK_ROPE), lambda i: (i, 0)),
                  pl.BlockSpec((1, QK_NOPE), lambda i: (0, 0)),
                  pl.BlockSpec((1, QK_ROPE), lambda i: (0, 0))],
        out_specs=pl.BlockSpec((n_heads, tm, QK_DIM), lambda i: (0, i, 0)),
        out_shape=jax.ShapeDtypeStruct((n_heads, m, QK_DIM), BF16),
        compiler_params=_params("parallel"),
        name="q_prep",
    )(p_q, g_qa, w_uq_h, cos2, sin2, gn, gr)


def _kv_prep_kernel(p_ref, g_ref, cos_ref, sin_ref, ckv_ref, kr_ref, *, kv_lora):
    x = p_ref[:, :kv_lora]
    ms = jnp.mean(x * x, axis=-1, keepdims=True)
    ckv_ref[...] = x * lax.rsqrt(ms + NORM_EPS) * g_ref[...]
    kr_ref[...] = _rope(p_ref[:, kv_lora:kv_lora + QK_ROPE], cos_ref[...], sin_ref[...])


def kv_prep(p_kvr, g_kva, cos2, sin2, kv_lora):
    m, wd = p_kvr.shape
    tm = _tile(m, 512, 8)
    return pl.pallas_call(
        functools.partial(_kv_prep_kernel, kv_lora=kv_lora),
        grid=(m // tm,),
        in_specs=[pl.BlockSpec((tm, wd), lambda i: (i, 0)),
                  pl.BlockSpec((1, kv_lora), lambda i: (0, 0)),
                  pl.BlockSpec((tm, QK_ROPE), lambda i: (i, 0)),
                  pl.BlockSpec((tm, QK_ROPE), lambda i: (i, 0))],
        out_specs=[pl.BlockSpec((tm, kv_lora), lambda i: (i, 0)),
                   pl.BlockSpec((tm, QK_ROPE), lambda i: (i, 0))],
        out_shape=[jax.ShapeDtypeStruct((m, kv_lora), F32),
                   jax.ShapeDtypeStruct((m, QK_ROPE), F32)],
        compiler_params=_params("parallel"),
        name="kv_prep",
    )(p_kvr, g_kva, cos2, sin2)


def _kv_expand_kernel(c_ref, kr_ref, w_ref, gn_ref, gr_ref, k_ref, v_ref, *, hb):
    hw = QK_NOPE + V_HD
    kv = jnp.dot(c_ref[...].astype(BF16), w_ref[...], preferred_element_type=F32)
    kr = kr_ref[...]
    ss_r = jnp.sum(kr * kr, axis=-1, keepdims=True)
    krg = kr * gr_ref[...]
    for h in range(hb):
        kn = kv[:, h * hw:h * hw + QK_NOPE]
        rstd = lax.rsqrt((jnp.sum(kn * kn, axis=-1, keepdims=True) + ss_r) * (1.0 / QK_DIM) + NORM_EPS)
        k_ref[h] = jnp.concatenate([kn * gn_ref[...] * rstd, krg * rstd], axis=1).astype(k_ref.dtype)
        v_ref[h] = kv[:, h * hw + QK_NOPE:(h + 1) * hw].astype(v_ref.dtype)


def kv_expand(ckv, krope, w_ukv, layer, gn, gr, n_heads, rows=None):
    kv_lora = ckv.shape[1]
    rws = ckv.shape[0] if rows is None else rows
    tr = _tile(rws, 1024, 16)
    hb = _tile(n_heads, 4, 1)
    hw = QK_NOPE + V_HD
    return pl.pallas_call(
        functools.partial(_kv_expand_kernel, hb=hb),
        grid=(rws // tr, n_heads // hb),
        in_specs=[pl.BlockSpec((tr, kv_lora), lambda i, h: (i, 0)),
                  pl.BlockSpec((tr, QK_ROPE), lambda i, h: (i, 0)),
                  _layer_spec((kv_lora, hb * hw), lambda i, h: (0, h), layer),
                  pl.BlockSpec((1, QK_NOPE), lambda i, h: (0, 0)),
                  pl.BlockSpec((1, QK_ROPE), lambda i, h: (0, 0))],
        out_specs=[pl.BlockSpec((hb, tr, QK_DIM), lambda i, h: (h, i, 0)),
                   pl.BlockSpec((hb, tr, V_HD), lambda i, h: (h, i, 0))],
        out_shape=[jax.ShapeDtypeStruct((n_heads, rws, QK_DIM), BF16),
                   jax.ShapeDtypeStruct((n_heads, rws, V_HD), BF16)],
        compiler_params=_params("parallel", "arbitrary"),
        name="kv_expand",
    )(ckv, krope, w_ukv, gn, gr)


def _attn_prompt_kernel(q_ref, k_ref, v_ref, o_ref, *, blk):
    i = pl.program_id(2)
    q = q_ref[0]
    half = blk // 2

    def step(qv, off, size, carry, mask):
        m, l, acc = carry
        s = lax.dot_general(qv, k_ref[0, pl.ds(off, size), :], NT, preferred_element_type=F32)
        if mask is not None:
            s = jnp.where(mask, s, -1e30)
        m_new = jnp.maximum(m, jnp.max(s, axis=-1, keepdims=True))
        p = jnp.exp(s - m_new)
        alpha = jnp.exp(m - m_new)
        l = alpha * l + jnp.sum(p, axis=-1, keepdims=True)
        acc = alpha * acc + jnp.dot(p.astype(BF16), v_ref[0, pl.ds(off, size), :], preferred_element_type=F32)
        return m_new, l, acc

    def chunk_mask(rows, cols):
        return (lax.broadcasted_iota(jnp.int32, (rows, cols), 1) // CHUNK
                <= lax.broadcasted_iota(jnp.int32, (rows, cols), 0) // CHUNK)

    init = (jnp.full((blk, 1), -1e30, F32), jnp.zeros((blk, 1), F32), jnp.zeros((blk, V_HD), F32))
    carry = lax.fori_loop(0, i, lambda j, c: step(q, pl.multiple_of(j * blk, blk), blk, c, None), init)
    off = pl.multiple_of(i * blk, blk)
    m, l, acc = step(q, off, half, carry, chunk_mask(blk, half))
    _, l_hi, acc_hi = step(q[half:], pl.multiple_of(off + half, half), half,
                           (m[half:], l[half:], acc[half:]), chunk_mask(half, half))
    l = jnp.concatenate([l[:half], l_hi], axis=0)
    acc = jnp.concatenate([acc[:half], acc_hi], axis=0)
    o_ref[...] = (acc / l).astype(o_ref.dtype)


def attn_prompt(q, k, v, *, n_seq, t_seq):
    n_heads = q.shape[0]
    blk = _tile(t_seq, 1024, CHUNK)
    nq = t_seq // blk
    return pl.pallas_call(
        functools.partial(_attn_prompt_kernel, blk=blk),
        grid=(n_seq, n_heads, nq),
        in_specs=[pl.BlockSpec((1, blk, QK_DIM), lambda b, h, i: (h, b * nq + i, 0)),
                  pl.BlockSpec((1, t_seq, QK_DIM), lambda b, h, i: (h, b, 0)),
                  pl.BlockSpec((1, t_seq, V_HD), lambda b, h, i: (h, b, 0))],
        out_specs=pl.BlockSpec((blk, V_HD), lambda b, h, i: (b * nq + i, h)),
        out_shape=jax.ShapeDtypeStruct((n_seq * t_seq, n_heads * V_HD), BF16),
        compiler_params=_params("parallel", "parallel", "arbitrary"),
        name="attn_prompt",
    )(q, k, v)


def _attn_sample_kernel(q_ref, k_ref, v_ref, o_ref, *, hb):
    for h in range(hb):
        s = lax.dot_general(q_ref[h], k_ref[h], NT, preferred_element_type=F32)
        m = jnp.max(s, axis=-1, keepdims=True)
        p = jnp.exp(s - m)
        l = jnp.sum(p, axis=-1, keepdims=True)
        acc = jnp.dot(p.astype(BF16), v_ref[h], preferred_element_type=F32)
        o_ref[:, h * V_HD:(h + 1) * V_HD] = (acc / l).astype(o_ref.dtype)


def attn_sample(q, k, v, *, row0, n_seq, t_q, t_kv):
    n_heads = q.shape[0]
    hb = _tile(n_heads, 4, 1)
    base = row0 // t_q
    return pl.pallas_call(
        functools.partial(_attn_sample_kernel, hb=hb),
        grid=(n_seq, n_heads // hb),
        in_specs=[pl.BlockSpec((hb, t_q, QK_DIM), lambda b, h: (h, base + b, 0)),
                  pl.BlockSpec((hb, t_kv, QK_DIM), lambda b, h: (h, b, 0)),
                  pl.BlockSpec((hb, t_kv, V_HD), lambda b, h: (h, b, 0))],
        out_specs=pl.BlockSpec((t_q, hb * V_HD), lambda b, h: (b, h)),
        out_shape=jax.ShapeDtypeStruct((n_seq * t_q, n_heads * V_HD), BF16),
        compiler_params=_params("parallel", "parallel"),
        name="attn_sample",
    )(q, k, v)


def _rope_tables(positions):
    half = QK_ROPE // 2
    inv_freq = ROPE_THETA ** (-jnp.arange(half, dtype=F32) / half)
    ang = positions.astype(F32)[:, None] * inv_freq[None, :]
    cos, sin = jnp.cos(ang), jnp.sin(ang)
    return jnp.concatenate([cos, cos], axis=1), jnp.concatenate([sin, sin], axis=1)


def _pad_cols(w, n):
    return jnp.pad(w, [(0, 0)] * (w.ndim - 1) + [(0, n - w.shape[-1])])


def kernel(x_prompt, x_sample, cache_ckv, cache_krope, state_wkv, state_shift, norm_attn_g, w_in, rw_mu, rw_w0, rw_w2, rw_a0, rw_a2, rw_g2, rw_k_k, rw_k_a, rw_r_k, rw_lnx_w, rw_lnx_b, mla_q_a_g, mla_w_uq, mla_kv_a_g, mla_w_ukv, mla_q_norm_g, mla_k_norm_g, w_out, norm_ffn_g, w_gate, w_up, w_down):
    bp, tp, d = x_prompt.shape
    bs, ts, _ = x_sample.shape
    depth = w_in.shape[0]
    past = cache_ckv.shape[2]
    kv_lora = cache_ckv.shape[3]
    rw_h = rw_r_k.shape[1]
    rw_w = rw_h * RW_HD
    w_lora, a_lora, g_lora = rw_w2.shape[1], rw_a2.shape[1], rw_g2.shape[1]
    lora = w_lora + a_lora + g_lora
    lora_w = -(-lora // LANE) * LANE
    rw_cols = 3 * rw_w + lora
    rw_pad = 3 * rw_w + lora_w
    q_lora = mla_w_uq.shape[1]
    mla_h = mla_w_uq.shape[2] // QK_DIM
    assert (3 * rw_w) % lora_w == 0 and tp % CHUNK == 0 and ts == CHUNK
    mp, ms = bp * tp, bs * ts
    m = mp + ms
    kvr_w = -(-(kv_lora + QK_ROPE) // LANE) * LANE

    layers = range(depth)
    w_rw = _pad_cols(w_in[:, :, :rw_cols], rw_pad).astype(BF16)
    w_q = [w_in[l, :, rw_cols:rw_cols + q_lora].astype(BF16) for l in layers]
    w_kvr = [_pad_cols(w_in[l, :, rw_cols + q_lora:], kvr_w).astype(BF16) for l in layers]
    mu_p = _pad_cols(rw_mu, rw_pad).reshape(depth, 1, rw_pad)
    zl = lambda w, r0: jnp.zeros((depth, lora_w, rw_w), F32).at[:, r0:r0 + w.shape[1]].set(w).astype(BF16)
    w2p, a2p, g2p = zl(rw_w2, 0), zl(rw_a2, w_lora), zl(rw_g2, w_lora + a_lora)
    vec = lambda t: t.reshape(depth, 1, -1)
    w_uq_h = [mla_w_uq[l].reshape(q_lora, mla_h, QK_DIM).transpose(1, 0, 2).astype(BF16) for l in layers]
    w_ukv, w_o = mla_w_ukv.astype(BF16), w_out.astype(BF16)
    w_g, w_u, w_d = w_gate.astype(BF16), w_up.astype(BF16), w_down.astype(BF16)
    qn_n, qn_r = mla_q_norm_g[:, None, :QK_NOPE], jnp.tile(mla_q_norm_g[:, None, QK_NOPE:], (1, 1, 2))
    kn_n, kn_r = mla_k_norm_g[:, None, :QK_NOPE], jnp.tile(mla_k_norm_g[:, None, QK_NOPE:], (1, 1, 2))

    pos = jnp.concatenate([jnp.tile(jnp.arange(tp), bp), jnp.tile(past + jnp.arange(ts), bs)])
    cos2, sin2 = _rope_tables(pos)

    x = jnp.concatenate([x_prompt.reshape(mp, d), x_sample.reshape(ms, d)], axis=0)
    zero_shift = jnp.zeros((bp, rw_pad), F32)
    to_pairs = lambda s: (s.reshape(-1, rw_h // 2, 2, RW_HD, RW_HD).transpose(0, 1, 4, 2, 3)
                          .reshape(-1, rw_h // 2, RW_HD, 2 * RW_HD))
    from_pairs = lambda p: (p.reshape(-1, rw_h // 2, RW_HD, 2, RW_HD).transpose(0, 1, 3, 4, 2)
                            .reshape(-1, rw_h, RW_HD, RW_HD))
    zero_state = jnp.zeros((bp, rw_h // 2, RW_HD, 2 * RW_HD), F32)
    outs = [[] for _ in range(8)]

    for l in range(depth):
        h = rmsnorm_bf16(x, norm_attn_g[l])
        p_rw, last = matmul_with_block_ends(h, w_rw, l)
        p_q = matmul(h, w_q[l], tn_pref=q_lora)
        p_kvr = matmul(h, w_kvr[l], tn_pref=kvr_w)

        lnw, lnb, rk = vec(rw_lnx_w)[l], vec(rw_lnx_b)[l], rw_r_k[l].reshape(1, rw_w)
        rw_args = (mu_p[l], vec(rw_w0)[l], vec(rw_a0)[l], vec(rw_k_k)[l], vec(rw_k_a)[l], w2p, a2p, g2p, l)
        rw_dims = dict(rw_w=rw_w, lora_w=lora_w, lora_dims=(w_lora, a_lora, g_lora))
        y_rw_p, pst_p = rwkv_mix(p_rw, last[:mp // CHUNK], zero_shift, *rw_args, zero_state, lnw, lnb, rk,
                                 row0=0, n_seq=bp, t_seq=tp, **rw_dims)
        y_rw_s, pst_s = rwkv_mix(p_rw, last[mp // CHUNK:], _pad_cols(state_shift[l], rw_pad), *rw_args,
                                 to_pairs(state_wkv[l]), lnw, lnb, rk, row0=mp, n_seq=bs, t_seq=ts, **rw_dims)

        q = q_prep(p_q, vec(mla_q_a_g)[l], w_uq_h[l], cos2, sin2, qn_n[l], qn_r[l])
        ckv_new, kr_new = kv_prep(p_kvr, vec(mla_kv_a_g)[l], cos2, sin2, kv_lora)
        k_p, v_p = kv_expand(ckv_new, kr_new, w_ukv, l, kn_n[l], kn_r[l], mla_h, rows=mp)
        y_mla_p = attn_prompt(q, k_p, v_p, n_seq=bp, t_seq=tp)
        ckv_s = jnp.concatenate([cache_ckv[l], ckv_new[mp:].reshape(bs, ts, kv_lora)], axis=1)
        kr_s = jnp.concatenate([cache_krope[l], kr_new[mp:].reshape(bs, ts, QK_ROPE)], axis=1)
        t_kv = past + ts
        k_s, v_s = kv_expand(ckv_s.reshape(bs * t_kv, kv_lora), kr_s.reshape(bs * t_kv, QK_ROPE),
                             w_ukv, l, kn_n[l], kn_r[l], mla_h)
        y_mla_s = attn_sample(q, k_s, v_s, row0=mp, n_seq=bs, t_q=ts, t_kv=t_kv)

        x = out_proj(y_rw_p, y_rw_s, y_mla_p, y_mla_s, w_o, l, x)

        h2 = rmsnorm_bf16(x, norm_ffn_g[l])
        u = ffn_up(h2, w_g, w_u, l)
        x = matmul_res_ksplit(u, w_d, l, x)

        outs[0].append(ckv_new[:mp].reshape(bp, tp, kv_lora))
        outs[1].append(kr_new[:mp].reshape(bp, tp, QK_ROPE))
        outs[2].append(from_pairs(pst_p))
        outs[3].append(p_rw[tp - 1:mp:tp, :rw_cols])
        outs[4].append(ckv_new[mp:].reshape(bs, ts, kv_lora))
        outs[5].append(kr_new[mp:].reshape(bs, ts, QK_ROPE))
        outs[6].append(from_pairs(pst_s))
        outs[7].append(p_rw[mp + ts - 1::ts, :rw_cols])

    return (x[:mp].reshape(bp, tp, d), x[mp:].reshape(bs, ts, d)) + tuple(jnp.stack(o) for o in outs)
```
